```python
import math
import jax, jax.numpy as jnp
from jax import lax
import numpy as np

D_MODEL = 1024
BATCH = 8
SEQ = 2048
DEPTH = 2

D_SSD = 1024
SSD_HEAD_DIM = 64
SSD_HEADS = D_SSD // SSD_HEAD_DIM
SSD_GROUPS = 4
SSD_STATE = 128
D_CONV = 5
CONV_CH = D_SSD + 2 * SSD_GROUPS * SSD_STATE
CHUNK = 128
D_FNO = 512
FNO_GROUPS = 4
FNO_GROUP_DIM = D_FNO // FNO_GROUPS
N_MEM = 256
MEM_HEADS = 4
D_MEM = 512
MEM_HEAD_DIM = D_MEM // MEM_HEADS
N_BRANCH = 3
EPS = 1e-6
IN_SIZES = (D_SSD, CONV_CH, 2 * SSD_HEADS, D_FNO, D_FNO, D_MEM, D_MEM, N_BRANCH * D_MODEL)
D_IN = sum(IN_SIZES)

kernel_name = "hybrid_ssd_fourier_memxattn_encoder"


def _offsets(sizes):
    out, acc = [], 0
    for s in sizes[:-1]:
        acc += s
        out.append(acc)
    return out


def rmsnorm(x, g):
    xf = x.astype(jnp.float32)
    y = xf * lax.rsqrt(jnp.mean(xf * xf, axis=-1, keepdims=True) + EPS)
    return (y * g.astype(jnp.float32)).astype(x.dtype)


def gated_group_rmsnorm(y, z, g):
    b, L, d = y.shape
    t = (y * jax.nn.silu(z)).astype(jnp.float32).reshape(b, L, SSD_GROUPS, d // SSD_GROUPS)
    t = t * lax.rsqrt(jnp.mean(t * t, axis=-1, keepdims=True) + EPS)
    return (t.reshape(b, L, d) * g.astype(jnp.float32)).astype(y.dtype)


def centred_depthwise_conv(u, w, bias):
    C = u.shape[-1]
    out = lax.conv_general_dilated(
        u, w.reshape(D_CONV, 1, C).astype(u.dtype), window_strides=(1,),
        padding=[(D_CONV // 2, D_CONV // 2)],
        dimension_numbers=("NWC", "WIO", "NWC"), feature_group_count=C)
    return out + bias.astype(u.dtype)


def segsum(a):
    T = a.shape[-1]
    a_rep = jnp.broadcast_to(a[..., :, None], a.shape + (T,))
    a_rep = jnp.where(jnp.tril(jnp.ones((T, T), bool), -1), a_rep, 0.0)
    out = jnp.cumsum(a_rep, axis=-2)
    return jnp.where(jnp.tril(jnp.ones((T, T), bool)), out, -jnp.inf)


def ssd_chunked(xdt, dtA, Bm, Cm):
    b, L, H, P = xdt.shape
    G, N = Bm.shape[-2:]
    R = H // G
    c = L // CHUNK
    X = xdt.reshape(b, c, CHUNK, G, R, P)
    A = dtA.reshape(b, c, CHUNK, G, R).transpose(0, 3, 4, 1, 2)
    Bc = Bm.reshape(b, c, CHUNK, G, N)
    Cc = Cm.reshape(b, c, CHUNK, G, N)
    A_cs = jnp.cumsum(A, axis=-1)
    Lmat = jnp.exp(segsum(A))
    CB = jnp.einsum("bclgn,bcsgn->bgcls", Cc, Bc)
    y_diag = jnp.einsum("bgcls,bgrcls,bcsgrp->bclgrp", CB, Lmat, X)
    decay_states = jnp.exp(A_cs[..., -1:] - A_cs)
    states = jnp.einsum("bclgn,bgrcl,bclgrp->bcgrpn", Bc, decay_states, X)
    states = jnp.concatenate([jnp.zeros_like(states[:, :1]), states], axis=1)
    chunk_decay = jnp.exp(segsum(jnp.pad(A_cs[..., -1], ((0, 0), (0, 0), (0, 0), (1, 0)))))
    prev_states = jnp.einsum("bgrzc,bcgrpn->bzgrpn", chunk_decay, states)[:, :-1]
    y_off = jnp.einsum("bclgn,bcgrpn,bgrcl->bclgrp", Cc, prev_states, jnp.exp(A_cs))
    return (y_diag + y_off).reshape(b, L, H, P)


def hybrid_layer(x, mem, norm_g, w_in, conv_w, conv_b, dt_bias, a_log, d_skip, ssd_norm_g,
                 mem_norm_g, w_kv, p_ssd, p_fno, p_mem, w_out):
    b, L, D = x.shape
    f32 = jnp.float32
    h = rmsnorm(x, norm_g)
    proj = h @ w_in
    z_s, xbc, dt_raw, u_f, z_f, q, z_m, gates = jnp.split(proj, _offsets(IN_SIZES), axis=-1)

    xbc = jax.nn.silu(centred_depthwise_conv(xbc, conv_w, conv_b))
    xs, Bm, Cm = jnp.split(xbc, [D_SSD, D_SSD + SSD_GROUPS * SSD_STATE], axis=-1)
    xs = xs.astype(f32).reshape(b, L, SSD_HEADS, SSD_HEAD_DIM)
    Bm = Bm.astype(f32).reshape(b, L, SSD_GROUPS, SSD_STATE)
    Cm = Cm.astype(f32).reshape(b, L, SSD_GROUPS, SSD_STATE)
    dt = jax.nn.softplus(dt_raw.astype(f32).reshape(b, L, 2, SSD_HEADS) + dt_bias.astype(f32))
    dtA = dt * (-jnp.exp(a_log.astype(f32)))
    flip = lambda t: jnp.flip(t, axis=1)
    y_fwd = ssd_chunked(xs * dt[:, :, 0, :, None], dtA[:, :, 0], Bm, Cm)
    y_bwd = flip(ssd_chunked(flip(xs * dt[:, :, 1, :, None]), flip(dtA[:, :, 1]), flip(Bm), flip(Cm)))
    y_s = (y_fwd + y_bwd + d_skip.astype(f32)[:, None] * xs).reshape(b, L, D_SSD).astype(x.dtype)
    y_s = gated_group_rmsnorm(y_s, z_s, ssd_norm_g)

    uf = u_f.astype(f32).reshape(b, L, FNO_GROUPS, FNO_GROUP_DIM)
    y_f = jnp.real(jnp.fft.fft2(uf, axes=(1, 3), norm="ortho")).reshape(b, L, D_FNO).astype(x.dtype)
    y_f = y_f * jax.nn.silu(z_f)

    kv = rmsnorm(mem, mem_norm_g) @ w_kv
    k, v = jnp.split(kv.reshape(b, N_MEM, 2, MEM_HEADS, MEM_HEAD_DIM), 2, axis=2)
    k, v = k[:, :, 0], v[:, :, 0]
    qh = q.reshape(b, L, MEM_HEADS, MEM_HEAD_DIM)
    s = jnp.einsum("blhd,bmhd->bhlm", qh, k).astype(f32) * (MEM_HEAD_DIM ** -0.5)
    pr = jax.nn.softmax(s, axis=-1).astype(x.dtype)
    y_m = jnp.einsum("bhlm,bmhd->blhd", pr, v).reshape(b, L, D_MEM) * jax.nn.silu(z_m)

    g = jax.nn.sigmoid(gates.astype(f32)).astype(x.dtype).reshape(b, L, N_BRANCH, D)
    merged = g[:, :, 0] * (y_s @ p_ssd) + g[:, :, 1] * (y_f @ p_fno) + g[:, :, 2] * (y_m @ p_mem)
    return x + merged @ w_out


def setup_inputs(seed: int = 0) -> dict:
    key = jax.random.key(seed)
    ks = jax.random.split(key, 20)
    nrm = lambda k, shape, fan_in: jax.random.normal(k, shape, jnp.float32) * (fan_in ** -0.5)
    x = jax.random.normal(ks[0], (BATCH, SEQ, D_MODEL), jnp.float32)
    mem = jax.random.normal(ks[1], (BATCH, N_MEM, D_MODEL), jnp.float32)
    norm_g = 1.0 + 0.02 * jax.random.normal(ks[2], (DEPTH, D_MODEL), jnp.float32)
    w_in = nrm(ks[3], (DEPTH, D_MODEL, D_IN), D_MODEL)
    conv_w = nrm(ks[4], (DEPTH, D_CONV, CONV_CH), D_CONV)
    conv_b = 0.02 * jax.random.normal(ks[5], (DEPTH, CONV_CH), jnp.float32)
    dt0 = jnp.exp(jax.random.uniform(ks[6], (DEPTH, 2, SSD_HEADS), jnp.float32,
                                     math.log(1e-3), math.log(1e-1)))
    dt_bias = dt0 + jnp.log(-jnp.expm1(-dt0))
    a_log = jnp.log(jax.random.uniform(ks[7], (DEPTH, 2, SSD_HEADS), jnp.float32, 1.0, 16.0))
    d_skip = 1.0 + 0.1 * jax.random.normal(ks[8], (DEPTH, SSD_HEADS), jnp.float32)
    ssd_norm_g = 1.0 + 0.02 * jax.random.normal(ks[9], (DEPTH, D_SSD), jnp.float32)
    mem_norm_g = 1.0 + 0.02 * jax.random.normal(ks[10], (DEPTH, D_MODEL), jnp.float32)
    w_kv = nrm(ks[11], (DEPTH, D_MODEL, 2 * D_MEM), D_MODEL)
    p_ssd = nrm(ks[12], (DEPTH, D_SSD, D_MODEL), D_SSD)
    p_fno = nrm(ks[13], (DEPTH, D_FNO, D_MODEL), D_FNO)
    p_mem = nrm(ks[14], (DEPTH, D_MEM, D_MODEL), D_MEM)
    w_out = nrm(ks[15], (DEPTH, D_MODEL, D_MODEL), D_MODEL)
    final_g = 1.0 + 0.02 * jax.random.normal(ks[16], (D_MODEL,), jnp.float32)
    return {"x": x, "mem": mem, "norm_g": norm_g, "w_in": w_in, "conv_w": conv_w,
            "conv_b": conv_b, "dt_bias": dt_bias, "a_log": a_log, "d_skip": d_skip,
            "ssd_norm_g": ssd_norm_g, "mem_norm_g": mem_norm_g, "w_kv": w_kv,
            "p_ssd": p_ssd, "p_fno": p_fno, "p_mem": p_mem, "w_out": w_out,
            "final_g": final_g}


def reference(x, mem, norm_g, w_in, conv_w, conv_b, dt_bias, a_log, d_skip, ssd_norm_g,
              mem_norm_g, w_kv, p_ssd, p_fno, p_mem, w_out, final_g):
    for i in range(DEPTH):
        x = hybrid_layer(x, mem, norm_g[i], w_in[i], conv_w[i], conv_b[i], dt_bias[i], a_log[i],
                         d_skip[i], ssd_norm_g[i], mem_norm_g[i], w_kv[i], p_ssd[i], p_fno[i],
                         p_mem[i], w_out[i])
    return rmsnorm(x, final_g)
```

```python
import functools
import math

import jax
import jax.numpy as jnp
from jax import lax
from jax.experimental import pallas as pl
from jax.experimental.pallas import tpu as pltpu

F32 = jnp.float32
BF16 = jnp.bfloat16
HIGHEST = lax.Precision.HIGHEST

D_MODEL = 1024
D_SSD = 1024
SSD_HEAD_DIM = 64
SSD_HEADS = D_SSD // SSD_HEAD_DIM
SSD_GROUPS = 4
HEADS_PER_GROUP = SSD_HEADS // SSD_GROUPS
GROUP_WIDTH = HEADS_PER_GROUP * SSD_HEAD_DIM
SSD_STATE = 128
BC_WIDTH = SSD_GROUPS * SSD_STATE
D_CONV = 5
CONV_CH = D_SSD + 2 * BC_WIDTH
CHUNK = 128
D_FNO = 512
FNO_GROUPS = 4
FNO_GROUP_DIM = D_FNO // FNO_GROUPS
N_MEM = 256
MEM_HEADS = 4
D_MEM = 512
MEM_HEAD_DIM = D_MEM // MEM_HEADS
N_BRANCH = 3
EPS = 1e-6
IN_SIZES = (D_SSD, CONV_CH, 2 * SSD_HEADS, D_FNO, D_FNO, D_MEM, D_MEM, N_BRANCH * D_MODEL)

SUBLANES = 8
CONV_HALO = D_CONV // 2
IN_PROJ_ROWS = 256
MIX_ROWS = 256
VMEM_LIMIT_BYTES = 56 * 1024 * 1024


def _sigmoid(v):
    return 0.5 * jnp.tanh(0.5 * v) + 0.5


def _silu(v):
    return v * _sigmoid(v)


def _softplus(v):
    return jnp.maximum(v, 0.0) + jnp.log1p(jnp.exp(-jnp.abs(v)))


def _rms_scale(v):
    return v * lax.rsqrt(jnp.mean(v * v, axis=-1, keepdims=True) + EPS)


def _dot(a, b):
    return jnp.dot(a, b, preferred_element_type=F32)


def _dot_nt(a, b):
    return lax.dot_general(a, b, (((1,), (1,)), ((), ())), preferred_element_type=F32)


def _dot_exact(a, b):
    return jnp.dot(a, b, precision=HIGHEST, preferred_element_type=F32)


def _in_proj_kernel(tiles_per_seq,
                    x_ref, xprev_ref, xnext_ref, g_ref,
                    wzs_ref, wxbc_ref, wdt_ref, wdtT_ref, wuf_ref, wzf_ref, wq_ref, wzm_ref,
                    wg_ref, convw_ref, convb_ref,
                    zs_ref, xsT_ref, b_ref, c_ref, dt_ref, dtT_ref, uf_ref, zf_ref, q_ref,
                    zm_ref, gates_ref,
                    stage_ref):
    rows = x_ref.shape[0]
    i = pl.program_id(0)
    lt = i % tiles_per_seq
    g = g_ref[...]
    h = (_rms_scale(x_ref[...]) * g).astype(BF16)

    zs_ref[...] = _dot(h, wzs_ref[...]).astype(BF16)
    uf_ref[...] = _dot(h, wuf_ref[...]).astype(BF16)
    zf_ref[...] = _dot(h, wzf_ref[...]).astype(BF16)
    q_ref[...] = _dot(h, wq_ref[...]).astype(BF16)
    zm_ref[...] = _dot(h, wzm_ref[...]).astype(BF16)
    gates_ref[...] = _dot(h, wg_ref[...]).astype(BF16)
    dt_ref[...] = _dot(h, wdt_ref[...])
    dtT_ref[...] = _dot_nt(wdtT_ref[...], h)

    keep_prev = (lt > 0).astype(F32)
    keep_next = (lt < tiles_per_seq - 1).astype(F32)
    h_prev = (_rms_scale(xprev_ref[...]) * g * keep_prev).astype(BF16)
    h_next = (_rms_scale(xnext_ref[...]) * g * keep_next).astype(BF16)
    w_xbc = wxbc_ref[...]
    stage_ref[0:SUBLANES, :] = _dot(h_prev, w_xbc)
    stage_ref[SUBLANES:SUBLANES + rows, :] = _dot(h, w_xbc)
    stage_ref[SUBLANES + rows:2 * SUBLANES + rows, :] = _dot(h_next, w_xbc)

    acc = convb_ref[...] + convw_ref[0:1, :] * stage_ref[pl.ds(SUBLANES - CONV_HALO, rows), :]
    for k in range(1, D_CONV):
        acc = acc + convw_ref[k:k + 1, :] * stage_ref[pl.ds(SUBLANES - CONV_HALO + k, rows), :]
    act = _silu(acc)
    xsT_ref[0] = act[:, :D_SSD].T.astype(BF16)
    b_ref[...] = act[:, D_SSD:D_SSD + BC_WIDTH].astype(BF16)
    c_ref[...] = act[:, D_SSD + BC_WIDTH:].astype(BF16)


def _const_spec(shape):
    nd = len(shape)
    return pl.BlockSpec(shape, lambda *_: (0,) * nd, pipeline_mode=pl.Buffered(1))


def _in_proj(x, norm_g, w_parts, conv_w, conv_b):
    bsz, seq, d = x.shape
    m = bsz * seq
    rows = IN_PROJ_ROWS
    tiles_per_seq = seq // rows
    n_tiles = m // rows
    sub_per_tile = rows // SUBLANES
    n_sub = m // SUBLANES
    x2 = x.reshape(m, d)
    (w_zs, w_xbc, w_dt, w_uf, w_zf, w_q, w_zm, w_g) = w_parts
    w_dtT = w_dt.T

    def row_spec(width):
        return pl.BlockSpec((rows, width), lambda i: (i, 0))

    in_specs = [
        row_spec(d),
        pl.BlockSpec((SUBLANES, d), lambda i: (jnp.maximum(i * sub_per_tile - 1, 0), 0)),
        pl.BlockSpec((SUBLANES, d), lambda i: (jnp.minimum((i + 1) * sub_per_tile, n_sub - 1), 0)),
        _const_spec((1, d)),
        _const_spec(w_zs.shape), _const_spec(w_xbc.shape), _const_spec(w_dt.shape),
        _const_spec(w_dtT.shape), _const_spec(w_uf.shape), _const_spec(w_zf.shape),
        _const_spec(w_q.shape), _const_spec(w_zm.shape), _const_spec(w_g.shape),
        _const_spec(conv_w.shape), _const_spec((1, CONV_CH)),
    ]
    out_shape = [
        jax.ShapeDtypeStruct((m, D_SSD), BF16),
        jax.ShapeDtypeStruct((bsz, D_SSD, seq), BF16),
        jax.ShapeDtypeStruct((m, BC_WIDTH), BF16),
        jax.ShapeDtypeStruct((m, BC_WIDTH), BF16),
        jax.ShapeDtypeStruct((m, 2 * SSD_HEADS), F32),
        jax.ShapeDtypeStruct((2 * SSD_HEADS, m), F32),
        jax.ShapeDtypeStruct((m, D_FNO), BF16),
        jax.ShapeDtypeStruct((m, D_FNO), BF16),
        jax.ShapeDtypeStruct((m, D_MEM), BF16),
        jax.ShapeDtypeStruct((m, D_MEM), BF16),
        jax.ShapeDtypeStruct((m, N_BRANCH * D_MODEL), BF16),
    ]
    out_specs = [
        row_spec(D_SSD),
        pl.BlockSpec((1, D_SSD, rows), lambda i: (i // tiles_per_seq, 0, i % tiles_per_seq)),
        row_spec(BC_WIDTH), row_spec(BC_WIDTH),
        row_spec(2 * SSD_HEADS),
        pl.BlockSpec((2 * SSD_HEADS, rows), lambda i: (0, i)),
        row_spec(D_FNO), row_spec(D_FNO), row_spec(D_MEM), row_spec(D_MEM),
        row_spec(N_BRANCH * D_MODEL),
    ]
    return pl.pallas_call(
        functools.partial(_in_proj_kernel, tiles_per_seq),
        grid=(n_tiles,),
        in_specs=in_specs,
        out_specs=out_specs,
        out_shape=out_shape,
        scratch_shapes=[pltpu.VMEM((rows + 2 * SUBLANES, CONV_CH), F32)],
        compiler_params=pltpu.CompilerParams(
            dimension_semantics=("arbitrary",), vmem_limit_bytes=VMEM_LIMIT_BYTES),
        name="in_proj",
    )(x2, x2, x2, norm_g.reshape(1, d), w_zs, w_xbc, w_dt, w_dtT, w_uf, w_zf, w_q, w_zm, w_g,
      conv_w, conv_b.reshape(1, CONV_CH))


def _expand_rows(v):
    lanes = v.shape[1]
    return jnp.concatenate(
        [jnp.broadcast_to(v[j:j + 1, :], (SSD_HEAD_DIM, lanes)) for j in range(v.shape[0])], axis=0)


def _ssd_kernel(n_chunks,
                xsT_ref, b_ref, c_ref, dt_ref, dtT_ref, bias_row_ref, alog_row_ref,
                bias_col_ref, alog_col_ref, dskip_ref,
                y_ref,
                snap_ref, state_ref):
    H = SSD_HEADS
    row_id = lax.broadcasted_iota(jnp.int32, (CHUNK, CHUNK), 0)
    col_id = lax.broadcasted_iota(jnp.int32, (CHUNK, CHUNK), 1)
    lower = row_id >= col_id
    upper = row_id <= col_id
    strict_lower = row_id > col_id
    strict_upper = row_id < col_id
    lower_f = lower.astype(F32)
    upper_f = upper.astype(F32)
    bias_col = bias_col_ref[...]
    neg_a_col = -jnp.exp(alog_col_ref[...])
    bias_row = bias_row_ref[...]
    neg_a_row = -jnp.exp(alog_row_ref[...])

    def lane_vectors(c):
        dt_t = _softplus(dtT_ref[:, pl.ds(c * CHUNK, CHUNK)] + bias_col)
        return dt_t, dt_t * neg_a_col

    state_ref[...] = jnp.zeros_like(state_ref)

    def fwd_body(c, carry):
        snap_ref[c] = state_ref[...].astype(BF16)
        dt_t, a_t = lane_vectors(c)
        cum = _dot_exact(a_t[:H], upper_f)
        total = cum[:, CHUNK - 1:CHUNK]
        w_t = dt_t[:H] * jnp.exp(total - cum)
        decay = jnp.exp(total)
        for g in range(SSD_GROUPS):
            hs = slice(g * HEADS_PER_GROUP, (g + 1) * HEADS_PER_GROUP)
            rs = slice(g * GROUP_WIDTH, (g + 1) * GROUP_WIDTH)
            xs_t = xsT_ref[0, rs, pl.ds(c * CHUNK, CHUNK)].astype(F32)
            xd_t = (xs_t * _expand_rows(w_t[hs])).astype(BF16)
            b_g = b_ref[0, pl.ds(c * CHUNK, CHUNK), g * SSD_STATE:(g + 1) * SSD_STATE]
            decay_g = _expand_rows(jnp.broadcast_to(decay[hs], (HEADS_PER_GROUP, SSD_STATE)))
            state_ref[rs, :] = decay_g * state_ref[rs, :] + _dot(xd_t, b_g)
        return carry

    lax.fori_loop(0, n_chunks, fwd_body, 0)

    state_ref[...] = jnp.zeros_like(state_ref)

    def bwd_body(it, carry):
        c = n_chunks - 1 - it
        dt_t, a_t = lane_vectors(c)
        cum_f_t = _dot_exact(a_t[:H], upper_f)
        cum_b_t = _dot_exact(a_t[H:], lower_f)
        dt_f_t, dt_b_t = dt_t[:H], dt_t[H:]
        e_f_t = jnp.exp(cum_f_t)
        e_b_t = jnp.exp(cum_b_t)
        total_b = cum_b_t[:, 0:1]
        w_b_t = dt_b_t * jnp.exp(total_b - cum_b_t)
        decay_b = jnp.exp(total_b)
        a_s = _softplus(dt_ref[0, pl.ds(c * CHUNK, CHUNK), :] + bias_row) * neg_a_row
        cum_f_s = _dot_exact(lower_f, a_s[:, :H])
        cum_b_s = _dot_exact(upper_f, a_s[:, H:])

        for g in range(SSD_GROUPS):
            hs = slice(g * HEADS_PER_GROUP, (g + 1) * HEADS_PER_GROUP)
            rs = slice(g * GROUP_WIDTH, (g + 1) * GROUP_WIDTH)
            b_g = b_ref[0, pl.ds(c * CHUNK, CHUNK), g * SSD_STATE:(g + 1) * SSD_STATE]
            c_g = c_ref[0, pl.ds(c * CHUNK, CHUNK), g * SSD_STATE:(g + 1) * SSD_STATE]
            xs_bf = xsT_ref[0, rs, pl.ds(c * CHUNK, CHUNK)]
            xs_t = xs_bf.astype(F32)
            cb = _dot_nt(c_g, b_g)
            y_parts = []
            for j in range(HEADS_PER_GROUP):
                hd = g * HEADS_PER_GROUP + j
                arg = jnp.where(lower,
                                cum_f_s[:, hd:hd + 1] - cum_f_t[hd:hd + 1, :],
                                cum_b_s[:, hd:hd + 1] - cum_b_t[hd:hd + 1, :])
                dt_mix = jnp.where(strict_lower, dt_f_t[hd:hd + 1, :],
                                   jnp.where(strict_upper, dt_b_t[hd:hd + 1, :],
                                             dt_f_t[hd:hd + 1, :] + dt_b_t[hd:hd + 1, :]))
                m_h = (cb * jnp.exp(arg) * dt_mix).astype(BF16)
                y_parts.append(_dot_nt(xs_bf[j * SSD_HEAD_DIM:(j + 1) * SSD_HEAD_DIM, :], m_h))
            states = jnp.concatenate([snap_ref[c, rs, :], state_ref[rs, :].astype(BF16)], axis=0)
            y_off = _dot_nt(states, c_g)
            y_t = (jnp.concatenate(y_parts, axis=0)
                   + y_off[:GROUP_WIDTH] * _expand_rows(e_f_t[hs])
                   + y_off[GROUP_WIDTH:] * _expand_rows(e_b_t[hs])
                   + dskip_ref[rs, :] * xs_t)
            y_ref[0, pl.ds(c * CHUNK, CHUNK), rs] = y_t.T.astype(BF16)
            xd_t = (xs_t * _expand_rows(w_b_t[hs])).astype(BF16)
            decay_g = _expand_rows(jnp.broadcast_to(decay_b[hs], (HEADS_PER_GROUP, SSD_STATE)))
            state_ref[rs, :] = decay_g * state_ref[rs, :] + _dot(xd_t, b_g)
        return carry

    lax.fori_loop(0, n_chunks, bwd_body, 0)


def _ssd(xsT, b_m, c_m, dt, dtT, dt_bias, a_log, d_skip):
    bsz, _, seq = xsT.shape
    n_chunks = seq // CHUNK
    two_h = 2 * SSD_HEADS
    bias_flat = dt_bias.reshape(two_h).astype(F32)
    alog_flat = a_log.reshape(two_h).astype(F32)
    bias_col = jnp.broadcast_to(bias_flat[:, None], (two_h, CHUNK))
    alog_col = jnp.broadcast_to(alog_flat[:, None], (two_h, CHUNK))
    dskip_col = jnp.broadcast_to(
        jnp.repeat(d_skip.astype(F32), SSD_HEAD_DIM)[:, None], (D_SSD, CHUNK))
    return pl.pallas_call(
        functools.partial(_ssd_kernel, n_chunks),
        grid=(bsz,),
        in_specs=[
            pl.BlockSpec((1, D_SSD, seq), lambda b: (b, 0, 0)),
            pl.BlockSpec((1, seq, BC_WIDTH), lambda b: (b, 0, 0)),
            pl.BlockSpec((1, seq, BC_WIDTH), lambda b: (b, 0, 0)),
            pl.BlockSpec((1, seq, two_h), lambda b: (b, 0, 0)),
            pl.BlockSpec((two_h, seq), lambda b: (0, b)),
            _const_spec((1, two_h)), _const_spec((1, two_h)),
            _const_spec((two_h, CHUNK)), _const_spec((two_h, CHUNK)),
            _const_spec((D_SSD, CHUNK)),
        ],
        out_specs=pl.BlockSpec((1, seq, D_SSD), lambda b: (b, 0, 0)),
        out_shape=jax.ShapeDtypeStruct((bsz, seq, D_SSD), BF16),
        scratch_shapes=[pltpu.VMEM((n_chunks, D_SSD, SSD_STATE), BF16),
                        pltpu.VMEM((D_SSD, SSD_STATE), F32)],
        compiler_params=pltpu.CompilerParams(
            dimension_semantics=("arbitrary",), vmem_limit_bytes=VMEM_LIMIT_BYTES),
        name="ssd",
    )(xsT, b_m.reshape(bsz, seq, BC_WIDTH), c_m.reshape(bsz, seq, BC_WIDTH),
      dt.reshape(bsz, seq, two_h), dtT, bias_flat.reshape(1, two_h), alog_flat.reshape(1, two_h),
      bias_col, alog_col, dskip_col)


def _mix_kernel(apply_final_norm,
                x_ref, ys_ref, zs_ref, ssdg_ref, uf_ref, zf_ref, q_ref, zm_ref, gates_ref,
                mem_ref, memg_ref, wkv_ref, chan_dft_ref, seq_dft_ref,
                pssd_ref, pfno_ref, pmem_ref, wout_ref, finalg_ref,
                out_ref,
                ab_ref, kv_ref):
    seq = uf_ref.shape[1]
    lt = pl.program_id(1)

    @pl.when(lt == 0)
    def _per_sequence():
        for gi in range(FNO_GROUPS):
            cs = slice(gi * FNO_GROUP_DIM, (gi + 1) * FNO_GROUP_DIM)
            t = _dot(uf_ref[0, :, cs], chan_dft_ref[...])
            ab_ref[0:seq, cs] = t[:, :FNO_GROUP_DIM].astype(BF16)
            ab_ref[seq:2 * seq, cs] = t[:, FNO_GROUP_DIM:].astype(BF16)
        hm = (_rms_scale(mem_ref[0]) * memg_ref[...]).astype(BF16)
        kv_ref[...] = _dot(hm, wkv_ref[...]).astype(BF16)

    t = ys_ref[0].astype(F32) * _silu(zs_ref[0].astype(F32))
    parts = []
    for gi in range(SSD_GROUPS):
        parts.append(_rms_scale(t[:, gi * GROUP_WIDTH:(gi + 1) * GROUP_WIDTH]))
    ysn = (jnp.concatenate(parts, axis=1) * ssdg_ref[...]).astype(BF16)
    m_s = _dot(ysn, pssd_ref[...])

    yf = _dot(seq_dft_ref[...], ab_ref[...]) * _silu(zf_ref[0].astype(F32))
    m_f = _dot(yf.astype(BF16), pfno_ref[...])

    q = q_ref[0]
    heads = []
    for hd in range(MEM_HEADS):
        cs = slice(hd * MEM_HEAD_DIM, (hd + 1) * MEM_HEAD_DIM)
        s = _dot_nt(q[:, cs], kv_ref[:, cs]) * (MEM_HEAD_DIM ** -0.5)
        p = jnp.exp(s - jnp.max(s, axis=-1, keepdims=True))
        denom = jnp.sum(p, axis=-1, keepdims=True)
        pv = _dot(p.astype(BF16), kv_ref[:, D_MEM + hd * MEM_HEAD_DIM:D_MEM + (hd + 1) * MEM_HEAD_DIM])
        heads.append(pv / denom)
    ym = jnp.concatenate(heads, axis=1) * _silu(zm_ref[0].astype(F32))
    m_m = _dot(ym.astype(BF16), pmem_ref[...])

    gate = _sigmoid(gates_ref[0].astype(F32))
    merged = (gate[:, :D_MODEL] * m_s + gate[:, D_MODEL:2 * D_MODEL] * m_f
              + gate[:, 2 * D_MODEL:] * m_m)
    out = x_ref[0] + _dot(merged.astype(BF16), wout_ref[...])
    if apply_final_norm:
        out = _rms_scale(out) * finalg_ref[...]
    out_ref[0] = out


def _mix(x, ys, zs, ssd_norm_g, uf, zf, q, zm, gates, mem, mem_norm_g, w_kv, chan_dft, seq_dft,
         p_ssd, p_fno, p_mem, w_out, final_g, apply_final_norm):
    bsz, seq, d = x.shape
    rows = MIX_ROWS
    n_tiles = seq // rows

    def tile_spec(width):
        return pl.BlockSpec((1, rows, width), lambda b, t: (b, t, 0))

    def r3(a, width):
        return a.reshape(bsz, seq, width)

    in_specs = [
        tile_spec(d), tile_spec(D_SSD), tile_spec(D_SSD), _const_spec((1, D_SSD)),
        pl.BlockSpec((1, seq, D_FNO), lambda b, t: (b, 0, 0)),
        tile_spec(D_FNO), tile_spec(D_MEM), tile_spec(D_MEM), tile_spec(N_BRANCH * D_MODEL),
        pl.BlockSpec((1, N_MEM, d), lambda b, t: (b, 0, 0)),
        _const_spec((1, d)), _const_spec(w_kv.shape), _const_spec(chan_dft.shape),
        pl.BlockSpec((rows, 2 * seq), lambda b, t: (t, 0)),
        _const_spec(p_ssd.shape), _const_spec(p_fno.shape), _const_spec(p_mem.shape),
        _const_spec(w_out.shape), _const_spec((1, d)),
    ]
    return pl.pallas_call(
        functools.partial(_mix_kernel, apply_final_norm),
        grid=(bsz, n_tiles),
        in_specs=in_specs,
        out_specs=tile_spec(d),
        out_shape=jax.ShapeDtypeStruct((bsz, seq, d), F32),
        scratch_shapes=[pltpu.VMEM((2 * seq, D_FNO), BF16),
                        pltpu.VMEM((N_MEM, 2 * D_MEM), BF16)],
        compiler_params=pltpu.CompilerParams(
            dimension_semantics=("arbitrary", "arbitrary"), vmem_limit_bytes=VMEM_LIMIT_BYTES),
        name="mix",
    )(x, ys, r3(zs, D_SSD), ssd_norm_g.reshape(1, D_SSD), r3(uf, D_FNO), r3(zf, D_FNO),
      r3(q, D_MEM), r3(zm, D_MEM), r3(gates, N_BRANCH * D_MODEL), mem,
      mem_norm_g.reshape(1, d), w_kv, chan_dft, seq_dft, p_ssd, p_fno, p_mem, w_out,
      final_g.reshape(1, d))


def _dft_cos_sin(n, split):
    m = jnp.arange(n, dtype=jnp.int32)
    k_hi = jnp.arange(n // split, dtype=jnp.int32)
    k_lo = jnp.arange(split, dtype=jnp.int32)
    ang_hi = (2.0 * math.pi / (n // split)) * ((k_hi[:, None] * m[None, :]) % (n // split)).astype(F32)
    ang_lo = (2.0 * math.pi / n) * ((k_lo[:, None] * m[None, :]) % n).astype(F32)
    ca, sa = jnp.cos(ang_hi)[:, None, :], jnp.sin(ang_hi)[:, None, :]
    cb, sb = jnp.cos(ang_lo)[None, :, :], jnp.sin(ang_lo)[None, :, :]
    cos = (ca * cb - sa * sb).reshape(n, n)
    sin = (sa * cb + ca * sb).reshape(n, n)
    return cos, sin


def _dft_tables(seq):
    cos_c, sin_c = _dft_cos_sin(FNO_GROUP_DIM, 8)
    chan = (jnp.concatenate([cos_c, sin_c], axis=1) * (FNO_GROUP_DIM ** -0.5)).astype(BF16)
    cos_l, sin_l = _dft_cos_sin(seq, 64)
    seq_tab = (jnp.concatenate([cos_l, -sin_l], axis=1) * (seq ** -0.5)).astype(BF16)
    return chan, seq_tab


def _split_w_in(w_in):
    parts, acc = [], 0
    for s in IN_SIZES:
        parts.append(w_in[:, acc:acc + s])
        acc += s
    return parts


def _layer(x, mem, norm_g, w_in, conv_w, conv_b, dt_bias, a_log, d_skip, ssd_norm_g, mem_norm_g,
           w_kv, p_ssd, p_fno, p_mem, w_out, final_g, tables, apply_final_norm):
    w_zs, w_xbc, w_dt, w_uf, w_zf, w_q, w_zm, w_g = [w.astype(BF16) for w in _split_w_in(w_in)]
    (zs, xsT, b_m, c_m, dt, dtT, uf, zf, q, zm, gates) = _in_proj(
        x, norm_g, (w_zs, w_xbc, w_dt, w_uf, w_zf, w_q, w_zm, w_g), conv_w, conv_b)
    ys = _ssd(xsT, b_m, c_m, dt, dtT, dt_bias, a_log, d_skip)
    chan_dft, seq_dft = tables
    return _mix(x, ys, zs, ssd_norm_g, uf, zf, q, zm, gates, mem, mem_norm_g, w_kv.astype(BF16),
                chan_dft, seq_dft, p_ssd.astype(BF16), p_fno.astype(BF16), p_mem.astype(BF16),
                w_out.astype(BF16), final_g, apply_final_norm)


def kernel(x, mem, norm_g, w_in, conv_w, conv_b, dt_bias, a_log, d_skip, ssd_norm_g, mem_norm_g,
           w_kv, p_ssd, p_fno, p_mem, w_out, final_g):
    depth = norm_g.shape[0]
    tables = _dft_tables(x.shape[1])
    for i in range(depth):
        x = _layer(x, mem, norm_g[i], w_in[i], conv_w[i], conv_b[i], dt_bias[i], a_log[i],
                   d_skip[i], ssd_norm_g[i], mem_norm_g[i], w_kv[i], p_ssd[i], p_fno[i],
                   p_mem[i], w_out[i], final_g, tables, i == depth - 1)
    return x
```

```python
import functools
import math

import jax
import jax.numpy as jnp
from jax import lax
from jax.experimental import pallas as pl
from jax.experimental.pallas import tpu as pltpu

F32 = jnp.float32
BF16 = jnp.bfloat16
HIGHEST = lax.Precision.HIGHEST

D_MODEL = 1024
D_SSD = 1024
SSD_HEAD_DIM = 64
SSD_HEADS = D_SSD // SSD_HEAD_DIM
SSD_GROUPS = 4
HEADS_PER_GROUP = SSD_HEADS // SSD_GROUPS
GROUP_WIDTH = HEADS_PER_GROUP * SSD_HEAD_DIM
SSD_STATE = 128
BC_WIDTH = SSD_GROUPS * SSD_STATE
D_CONV = 5
CONV_CH = D_SSD + 2 * BC_WIDTH
CHUNK = 128
D_FNO = 512
FNO_GROUPS = 4
FNO_GROUP_DIM = D_FNO // FNO_GROUPS
N_MEM = 256
MEM_HEADS = 4
D_MEM = 512
MEM_HEAD_DIM = D_MEM // MEM_HEADS
N_BRANCH = 3
EPS = 1e-6
IN_SIZES = (D_SSD, CONV_CH, 2 * SSD_HEADS, D_FNO, D_FNO, D_MEM, D_MEM, N_BRANCH * D_MODEL)
REST_WIDTH = sum(IN_SIZES) - CONV_CH - 2 * SSD_HEADS
XBC_STEP = 256
REST_STEP = REST_WIDTH // (CONV_CH // XBC_STEP)

SUBLANES = 8
HALO_ROWS = 16
CONV_ROWS = 64
CONV_STRIP = 128
CONV_HALO = D_CONV // 2
IN_PROJ_ROWS = 512
MIX_ROWS = 512
VMEM_LIMIT_BYTES = 56 * 1024 * 1024


def _sigmoid(v):
    return 0.5 * jnp.tanh(0.5 * v) + 0.5


def _silu(v):
    return v * _sigmoid(v)


def _softplus(v):
    return jnp.maximum(v, 0.0) + jnp.log1p(jnp.exp(-jnp.abs(v)))


def _rms_scale(v):
    return v * lax.rsqrt(jnp.mean(v * v, axis=-1, keepdims=True) + EPS)


def _dot(a, b):
    return jnp.dot(a, b, preferred_element_type=F32)


def _dot_nt(a, b):
    return lax.dot_general(a, b, (((1,), (1,)), ((), ())), preferred_element_type=F32)


def _dot_exact(a, b):
    return jnp.dot(a, b, precision=HIGHEST, preferred_element_type=F32)


def _in_proj_kernel(tiles_per_seq,
                    x_ref, xprev_ref, xnext_ref, g_ref,
                    wxbc_ref, wrest_ref, wdtT_ref, convw_ref, convb_ref,
                    rest_ref, xsT_ref, bc_ref, dtT_ref,
                    stage_ref, xs_ref, h_ref):
    rows = x_ref.shape[0]
    lt = pl.program_id(0) % tiles_per_seq
    g = g_ref[...]
    keep_prev = (lt > 0).astype(F32)
    keep_next = (lt < tiles_per_seq - 1).astype(F32)
    h_ref[0:HALO_ROWS, :] = (_rms_scale(xprev_ref[...]) * g * keep_prev).astype(BF16)
    h_ref[HALO_ROWS:HALO_ROWS + rows, :] = (_rms_scale(x_ref[...]) * g).astype(BF16)
    h_ref[HALO_ROWS + rows:, :] = (_rms_scale(xnext_ref[...]) * g * keep_next).astype(BF16)
    h_own = h_ref.at[HALO_ROWS:HALO_ROWS + rows, :]

    def project_xbc(j):
        cols = pl.ds(j * XBC_STEP, XBC_STEP)
        stage_ref[:, cols] = _dot(h_ref[...], wxbc_ref[:, cols])

    def project_rest(j):
        cols = pl.ds(j * REST_STEP, REST_STEP)
        rest_ref[:, cols] = _dot(h_own[...], wrest_ref[:, cols]).astype(BF16)

    def conv(j, is_x):
        win_rows = CONV_ROWS + 2 * SUBLANES
        for t in range(XBC_STEP // CONV_STRIP):
            col = j * XBC_STEP + t * CONV_STRIP
            cs = pl.ds(col, CONV_STRIP)
            bias = convb_ref[:, cs]
            taps = [convw_ref[k:k + 1, cs] for k in range(D_CONV)]
            for r0 in range(0, rows, CONV_ROWS):
                lo = HALO_ROWS + r0 - SUBLANES
                win = stage_ref[lo:lo + win_rows, cs]
                acc = bias + taps[CONV_HALO] * win[SUBLANES:SUBLANES + CONV_ROWS]
                for k in range(D_CONV):
                    d = k - CONV_HALO
                    if d != 0:
                        shifted = pltpu.roll(win, (win_rows - d) % win_rows, axis=0)
                        acc = acc + taps[k] * shifted[SUBLANES:SUBLANES + CONV_ROWS]
                act = _silu(acc)
                if is_x:
                    xs_ref[r0:r0 + CONV_ROWS, t * CONV_STRIP:(t + 1) * CONV_STRIP] = act
                else:
                    bc_ref[r0:r0 + CONV_ROWS, pl.ds(col - D_SSD, CONV_STRIP)] = act.astype(BF16)
            if is_x:
                for r0 in range(0, rows, CHUNK):
                    blk = xs_ref[r0:r0 + CHUNK, t * CONV_STRIP:(t + 1) * CONV_STRIP]
                    xsT_ref[0, cs, r0:r0 + CHUNK] = blk.T.astype(BF16)

    project_xbc(0)
    dtT_ref[...] = _dot_nt(wdtT_ref[...], h_own[...])

    n_steps = CONV_CH // XBC_STEP
    for j in range(n_steps):
        conv(j, j * XBC_STEP < D_SSD)
        if j + 1 < n_steps:
            project_xbc(j + 1)
        project_rest(j)


def _const_spec(shape):
    nd = len(shape)
    return pl.BlockSpec(shape, lambda *_: (0,) * nd, pipeline_mode=pl.Buffered(1))


def _in_proj(x, norm_g, w_xbc, w_rest, w_dtT, conv_w, conv_b):
    bsz, seq, d = x.shape
    m = bsz * seq
    rows = IN_PROJ_ROWS
    tiles_per_seq = seq // rows
    n_tiles = m // rows
    halo_per_tile = rows // HALO_ROWS
    n_halo = m // HALO_ROWS
    x2 = x.reshape(m, d)

    def row_spec(width):
        return pl.BlockSpec((rows, width), lambda i: (i, 0))

    in_specs = [
        row_spec(d),
        pl.BlockSpec((HALO_ROWS, d), lambda i: (jnp.maximum(i * halo_per_tile - 1, 0), 0)),
        pl.BlockSpec((HALO_ROWS, d),
                     lambda i: (jnp.minimum((i + 1) * halo_per_tile, n_halo - 1), 0)),
        _const_spec((1, d)),
        _const_spec(w_xbc.shape), _const_spec(w_rest.shape), _const_spec(w_dtT.shape),
        _const_spec(conv_w.shape), _const_spec((1, CONV_CH)),
    ]
    out_shape = [
        jax.ShapeDtypeStruct((m, REST_WIDTH), BF16),
        jax.ShapeDtypeStruct((bsz, D_SSD, seq), BF16),
        jax.ShapeDtypeStruct((m, 2 * BC_WIDTH), BF16),
        jax.ShapeDtypeStruct((2 * SSD_HEADS, m), F32),
    ]
    out_specs = [
        row_spec(REST_WIDTH),
        pl.BlockSpec((1, D_SSD, rows), lambda i: (i // tiles_per_seq, 0, i % tiles_per_seq)),
        row_spec(2 * BC_WIDTH),
        pl.BlockSpec((2 * SSD_HEADS, rows), lambda i: (0, i)),
    ]
    return pl.pallas_call(
        functools.partial(_in_proj_kernel, tiles_per_seq),
        grid=(n_tiles,),
        in_specs=in_specs,
        out_specs=out_specs,
        out_shape=out_shape,
        scratch_shapes=[pltpu.VMEM((rows + 2 * HALO_ROWS, CONV_CH), F32),
                        pltpu.VMEM((rows, XBC_STEP), F32),
                        pltpu.VMEM((rows + 2 * HALO_ROWS, d), BF16)],
        compiler_params=pltpu.CompilerParams(
            dimension_semantics=("arbitrary",), vmem_limit_bytes=VMEM_LIMIT_BYTES),
        name="in_proj",
    )(x2, x2, x2, norm_g.reshape(1, d), w_xbc, w_rest, w_dtT, conv_w, conv_b.reshape(1, CONV_CH))


def _expand_rows(v):
    lanes = v.shape[1]
    return jnp.concatenate(
        [jnp.broadcast_to(v[j:j + 1, :], (SSD_HEAD_DIM, lanes)) for j in range(v.shape[0])], axis=0)


TAB_QF, TAB_QB, TAB_EF, TAB_EB, TAB_WF, TAB_WB, TAB_DECAY_F, TAB_DECAY_B, TAB_DTB = range(9)
LOG2E = 1.4426950408889634


def _ssd_kernel(n_chunks,
                xsT_ref, b_ref, c_ref, dtT_ref, bias_col_ref, alog_col_ref, dskip_ref,
                y_ref,
                snap_ref, state_ref, tab_ref, colt_ref):
    H = SSD_HEADS
    row_id = lax.broadcasted_iota(jnp.int32, (CHUNK, CHUNK), 0)
    col_id = lax.broadcasted_iota(jnp.int32, (CHUNK, CHUNK), 1)
    lower = row_id >= col_id
    diag = row_id == col_id
    lower_f = lower.astype(F32)
    upper_f = (row_id <= col_id).astype(F32)

    dt_all = _softplus(dtT_ref[...] + bias_col_ref[...])
    a_all = dt_all * -jnp.exp(alog_col_ref[...])

    def stack(v, lo):
        return jnp.concatenate(
            [v[lo:lo + H, c * CHUNK:(c + 1) * CHUNK] for c in range(n_chunks)], axis=0)

    dt_f, dt_b = stack(dt_all, 0), stack(dt_all, H)
    cum_f = _dot_exact(stack(a_all, 0), upper_f)
    cum_b = _dot_exact(stack(a_all, H), lower_f)
    total_f = jnp.broadcast_to(cum_f[:, CHUNK - 1:CHUNK], cum_f.shape)
    total_b = jnp.broadcast_to(cum_b[:, 0:1], cum_b.shape)
    p_f = cum_f * LOG2E
    p_b = cum_b * LOG2E
    tab_ref[TAB_QF] = p_f - jnp.log2(dt_f)
    tab_ref[TAB_QB] = p_b - jnp.log2(dt_b)
    tab_ref[TAB_EF] = jnp.exp(cum_f)
    tab_ref[TAB_EB] = jnp.exp(cum_b)
    tab_ref[TAB_WF] = dt_f * jnp.exp(total_f - cum_f)
    tab_ref[TAB_WB] = dt_b * jnp.exp(total_b - cum_b)
    tab_ref[TAB_DECAY_F] = jnp.exp(total_f)
    tab_ref[TAB_DECAY_B] = jnp.exp(total_b)
    tab_ref[TAB_DTB] = dt_b
    pad = jnp.zeros((CHUNK - 2 * H, CHUNK), F32)
    for c in range(n_chunks):
        rows = slice(c * H, (c + 1) * H)
        colt_ref[c] = jnp.concatenate([p_f[rows], p_b[rows], pad], axis=0).T

    def chunk_rows(k, c):
        return tab_ref[k, pl.ds(pl.multiple_of(c * H, H), H), :]

    def state_update(c, g, xs_t, w, decay):
        hs = slice(g * HEADS_PER_GROUP, (g + 1) * HEADS_PER_GROUP)
        rs = slice(g * GROUP_WIDTH, (g + 1) * GROUP_WIDTH)
        xd_t = (xs_t * _expand_rows(w[hs])).astype(BF16)
        b_g = b_ref[0, pl.ds(c * CHUNK, CHUNK), g * SSD_STATE:(g + 1) * SSD_STATE]
        state_ref[rs, :] = _expand_rows(decay[hs]) * state_ref[rs, :] + _dot(xd_t, b_g)

    state_ref[...] = jnp.zeros_like(state_ref)

    def fwd_body(c, carry):
        snap_ref[c] = state_ref[...].astype(BF16)
        w, decay = chunk_rows(TAB_WF, c), chunk_rows(TAB_DECAY_F, c)
        for g in range(SSD_GROUPS):
            rs = slice(g * GROUP_WIDTH, (g + 1) * GROUP_WIDTH)
            xs_t = xsT_ref[0, rs, pl.ds(c * CHUNK, CHUNK)].astype(F32)
            state_update(c, g, xs_t, w, decay)
        return carry

    lax.fori_loop(0, n_chunks, fwd_body, 0)

    state_ref[...] = jnp.zeros_like(state_ref)

    def bwd_body(it, carry):
        c = n_chunks - 1 - it
        q_f, q_b = chunk_rows(TAB_QF, c), chunk_rows(TAB_QB, c)
        e_f, e_b = chunk_rows(TAB_EF, c), chunk_rows(TAB_EB, c)
        w_b, decay_b = chunk_rows(TAB_WB, c), chunk_rows(TAB_DECAY_B, c)
        dt_b_c = chunk_rows(TAB_DTB, c)
        for g in range(SSD_GROUPS):
            hs = slice(g * HEADS_PER_GROUP, (g + 1) * HEADS_PER_GROUP)
            rs = slice(g * GROUP_WIDTH, (g + 1) * GROUP_WIDTH)
            b_g = b_ref[0, pl.ds(c * CHUNK, CHUNK), g * SSD_STATE:(g + 1) * SSD_STATE]
            c_g = c_ref[0, pl.ds(c * CHUNK, CHUNK), g * SSD_STATE:(g + 1) * SSD_STATE]
            xs_bf = xsT_ref[0, rs, pl.ds(c * CHUNK, CHUNK)]
            xs_t = xs_bf.astype(F32)
            cb = _dot_nt(c_g, b_g)
            y_parts = []
            for j in range(HEADS_PER_GROUP):
                hd = g * HEADS_PER_GROUP + j
                arg = jnp.where(lower,
                                colt_ref[c, :, hd:hd + 1] - q_f[hd:hd + 1, :],
                                colt_ref[c, :, H + hd:H + hd + 1] - q_b[hd:hd + 1, :])
                m_h = (cb * jnp.exp2(arg)).astype(BF16)
                y_parts.append(_dot_nt(xs_bf[j * SSD_HEAD_DIM:(j + 1) * SSD_HEAD_DIM, :], m_h))
            cb_diag = jnp.sum(jnp.where(diag, cb, 0.0), axis=0, keepdims=True)
            own = dskip_ref[rs, :] + _expand_rows(cb_diag * dt_b_c[hs])
            states = jnp.concatenate([snap_ref[c, rs, :], state_ref[rs, :].astype(BF16)], axis=0)
            y_off = _dot_nt(states, c_g)
            y_t = (jnp.concatenate(y_parts, axis=0)
                   + y_off[:GROUP_WIDTH] * _expand_rows(e_f[hs])
                   + y_off[GROUP_WIDTH:] * _expand_rows(e_b[hs])
                   + own * xs_t)
            y_ref[0, pl.ds(c * CHUNK, CHUNK), rs] = y_t.T.astype(BF16)
            state_update(c, g, xs_t, w_b, decay_b)
        return carry

    lax.fori_loop(0, n_chunks, bwd_body, 0)


def _ssd(xsT, bc, dtT, dt_bias, a_log, d_skip):
    bsz, _, seq = xsT.shape
    n_chunks = seq // CHUNK
    two_h = 2 * SSD_HEADS
    bias_col = jnp.broadcast_to(dt_bias.reshape(two_h, 1).astype(F32), (two_h, seq))
    alog_col = jnp.broadcast_to(a_log.reshape(two_h, 1).astype(F32), (two_h, seq))
    dskip_col = jnp.broadcast_to(
        jnp.repeat(d_skip.astype(F32), SSD_HEAD_DIM)[:, None], (D_SSD, CHUNK))
    bc3 = bc.reshape(bsz, seq, 2 * BC_WIDTH)
    return pl.pallas_call(
        functools.partial(_ssd_kernel, n_chunks),
        grid=(bsz,),
        in_specs=[
            pl.BlockSpec((1, D_SSD, seq), lambda b: (b, 0, 0)),
            pl.BlockSpec((1, seq, BC_WIDTH), lambda b: (b, 0, 0)),
            pl.BlockSpec((1, seq, BC_WIDTH), lambda b: (b, 0, 1)),
            pl.BlockSpec((two_h, seq), lambda b: (0, b)),
            _const_spec((two_h, seq)), _const_spec((two_h, seq)),
            _const_spec((D_SSD, CHUNK)),
        ],
        out_specs=pl.BlockSpec((1, seq, D_SSD), lambda b: (b, 0, 0)),
        out_shape=jax.ShapeDtypeStruct((bsz, seq, D_SSD), BF16),
        scratch_shapes=[pltpu.VMEM((n_chunks, D_SSD, SSD_STATE), BF16),
                        pltpu.VMEM((D_SSD, SSD_STATE), F32),
                        pltpu.VMEM((TAB_DTB + 1, n_chunks * SSD_HEADS, CHUNK), F32),
                        pltpu.VMEM((n_chunks, CHUNK, CHUNK), F32)],
        compiler_params=pltpu.CompilerParams(
            dimension_semantics=("arbitrary",), vmem_limit_bytes=VMEM_LIMIT_BYTES),
        name="ssd",
    )(xsT, bc3, bc3, dtT, bias_col, alog_col, dskip_col)


def _mix_kernel(apply_final_norm,
                x_ref, ys_ref, zs_ref, ssdg_ref, uf_ref, zf_ref, q_ref, zm_ref, gates_ref,
                mem_ref, memg_ref, wkv_ref, chan_dft_ref, seq_dft_ref,
                pssd_ref, pfno_ref, pmem_ref, wout_ref, finalg_ref,
                out_ref,
                ab_ref, kv_ref):
    seq = uf_ref.shape[1]
    lt = pl.program_id(1)

    @pl.when(lt == 0)
    def _per_sequence():
        for gi in range(FNO_GROUPS):
            cs = slice(gi * FNO_GROUP_DIM, (gi + 1) * FNO_GROUP_DIM)
            t = _dot(uf_ref[0, :, cs], chan_dft_ref[...])
            ab_ref[0:seq, cs] = t[:, :FNO_GROUP_DIM].astype(BF16)
            ab_ref[seq:2 * seq, cs] = t[:, FNO_GROUP_DIM:].astype(BF16)
        hm = (_rms_scale(mem_ref[0]) * memg_ref[...]).astype(BF16)
        kv_ref[...] = _dot(hm, wkv_ref[...]).astype(BF16)

    t = ys_ref[0].astype(F32) * _silu(zs_ref[0].astype(F32))
    parts = []
    for gi in range(SSD_GROUPS):
        parts.append(_rms_scale(t[:, gi * GROUP_WIDTH:(gi + 1) * GROUP_WIDTH]))
    ysn = (jnp.concatenate(parts, axis=1) * ssdg_ref[...]).astype(BF16)
    m_s = _dot(ysn, pssd_ref[...])

    yf = _dot(seq_dft_ref[...], ab_ref[...]) * _silu(zf_ref[0].astype(F32))
    m_f = _dot(yf.astype(BF16), pfno_ref[...])

    q = q_ref[0]
    heads = []
    for hd in range(MEM_HEADS):
        cs = slice(hd * MEM_HEAD_DIM, (hd + 1) * MEM_HEAD_DIM)
        s = _dot_nt(q[:, cs], kv_ref[:, cs]) * (MEM_HEAD_DIM ** -0.5)
        p = jnp.exp(s - jnp.max(s, axis=-1, keepdims=True))
        denom = jnp.sum(p, axis=-1, keepdims=True)
        pv = _dot(p.astype(BF16), kv_ref[:, D_MEM + hd * MEM_HEAD_DIM:D_MEM + (hd + 1) * MEM_HEAD_DIM])
        heads.append(pv / denom)
    ym = jnp.concatenate(heads, axis=1) * _silu(zm_ref[0].astype(F32))
    m_m = _dot(ym.astype(BF16), pmem_ref[...])

    gate = _sigmoid(gates_ref[0].astype(F32))
    merged = (gate[:, :D_MODEL] * m_s + gate[:, D_MODEL:2 * D_MODEL] * m_f
              + gate[:, 2 * D_MODEL:] * m_m)
    out = x_ref[0] + _dot(merged.astype(BF16), wout_ref[...])
    if apply_final_norm:
        out = _rms_scale(out) * finalg_ref[...]
    out_ref[0] = out


def _mix(x, ys, rest, ssd_norm_g, mem, mem_norm_g, w_kv, chan_dft, seq_dft,
         p_ssd, p_fno, p_mem, w_out, final_g, apply_final_norm):
    bsz, seq, d = x.shape
    rows = MIX_ROWS
    n_tiles = seq // rows
    rest3 = rest.reshape(bsz, seq, REST_WIDTH)

    def tile_spec(width, col_block=0):
        return pl.BlockSpec((1, rows, width), lambda b, t: (b, t, col_block))

    assert D_SSD == 2 * D_FNO and D_FNO == D_MEM and REST_WIDTH == 2 * N_BRANCH * D_MODEL
    in_specs = [
        tile_spec(d), tile_spec(D_SSD), tile_spec(D_SSD, 0), _const_spec((1, D_SSD)),
        pl.BlockSpec((1, seq, D_FNO), lambda b, t: (b, 0, 2)),
        tile_spec(D_FNO, 3), tile_spec(D_MEM, 4), tile_spec(D_MEM, 5),
        tile_spec(N_BRANCH * D_MODEL, 1),
        pl.BlockSpec((1, N_MEM, d), lambda b, t: (b, 0, 0)),
        _const_spec((1, d)), _const_spec(w_kv.shape), _const_spec(chan_dft.shape),
        pl.BlockSpec((rows, 2 * seq), lambda b, t: (t, 0)),
        _const_spec(p_ssd.shape), _const_spec(p_fno.shape), _const_spec(p_mem.shape),
        _const_spec(w_out.shape), _const_spec((1, d)),
    ]
    return pl.pallas_call(
        functools.partial(_mix_kernel, apply_final_norm),
        grid=(bsz, n_tiles),
        in_specs=in_specs,
        out_specs=tile_spec(d),
        out_shape=jax.ShapeDtypeStruct((bsz, seq, d), F32),
        scratch_shapes=[pltpu.VMEM((2 * seq, D_FNO), BF16),
                        pltpu.VMEM((N_MEM, 2 * D_MEM), BF16)],
        compiler_params=pltpu.CompilerParams(
            dimension_semantics=("arbitrary", "arbitrary"), vmem_limit_bytes=VMEM_LIMIT_BYTES),
        name="mix",
    )(x, ys, rest3, ssd_norm_g.reshape(1, D_SSD), rest3, rest3, rest3, rest3, rest3, mem,
      mem_norm_g.reshape(1, d), w_kv, chan_dft, seq_dft, p_ssd, p_fno, p_mem, w_out,
      final_g.reshape(1, d))


def _dft_table(n, split, sin_sign):
    m = jnp.arange(n, dtype=jnp.int32)
    k_hi = jnp.arange(n // split, dtype=jnp.int32) * split
    k_lo = jnp.arange(split, dtype=jnp.int32)
    ang_hi = (2.0 * math.pi / n) * ((k_hi[:, None] * m[None, :]) % n).astype(F32)
    ang_lo = (2.0 * math.pi / n) * ((k_lo[:, None] * m[None, :]) % n).astype(F32)
    ca, sa = jnp.cos(ang_hi), jnp.sin(ang_hi)
    cb, sb = jnp.cos(ang_lo)[None, :, None, :], jnp.sin(ang_lo)[None, :, None, :]
    u = jnp.stack([ca, sin_sign * sa], axis=1)[:, None, :, :]
    v = jnp.stack([-sa, sin_sign * ca], axis=1)[:, None, :, :]
    return ((u * cb + v * sb) * (n ** -0.5)).astype(BF16).reshape(n, 2 * n)


def _dft_tables(seq):
    return _dft_table(FNO_GROUP_DIM, 8, 1.0), _dft_table(seq, 64, -1.0)


def _layer(x, mem, norm_g, w_in, conv_w, conv_b, dt_bias, a_log, d_skip, ssd_norm_g, mem_norm_g,
           w_kv, p_ssd, p_fno, p_mem, w_out, final_g, tables, apply_final_norm):
    xbc_lo, dt_lo = IN_SIZES[0], IN_SIZES[0] + IN_SIZES[1]
    rest_lo = dt_lo + IN_SIZES[2]
    w_bf = w_in.astype(BF16)
    w_rest = jnp.concatenate([w_bf[:, :xbc_lo], w_bf[:, rest_lo:]], axis=1)
    rest, xsT, bc, dtT = _in_proj(x, norm_g, w_bf[:, xbc_lo:dt_lo], w_rest,
                                  w_bf[:, dt_lo:rest_lo].T, conv_w, conv_b)
    ys = _ssd(xsT, bc, dtT, dt_bias, a_log, d_skip)
    chan_dft, seq_dft = tables
    return _mix(x, ys, rest, ssd_norm_g, mem, mem_norm_g, w_kv.astype(BF16), chan_dft, seq_dft,
                p_ssd.astype(BF16), p_fno.astype(BF16), p_mem.astype(BF16), w_out.astype(BF16),
                final_g, apply_final_norm)


def kernel(x, mem, norm_g, w_in, conv_w, conv_b, dt_bias, a_log, d_skip, ssd_norm_g, mem_norm_g,
           w_kv, p_ssd, p_fno, p_mem, w_out, final_g):
    depth = norm_g.shape[0]
    tables = _dft_tables(x.shape[1])
    for i in range(depth):
        x = _layer(x, mem, norm_g[i], w_in[i], conv_w[i], conv_b[i], dt_bias[i], a_log[i],
                   d_skip[i], ssd_norm_g[i], mem_norm_g[i], w_kv[i], p_ssd[i], p_fno[i],
                   p_mem[i], w_out[i], final_g, tables, i == depth - 1)
    return x
```

```python
import functools
import math

import jax
import jax.numpy as jnp
from jax import lax
from jax.experimental import pallas as pl
from jax.experimental.pallas import tpu as pltpu

F32 = jnp.float32
BF16 = jnp.bfloat16
HIGHEST = lax.Precision.HIGHEST

D_MODEL = 1024
D_SSD = 1024
SSD_HEAD_DIM = 64
SSD_HEADS = D_SSD // SSD_HEAD_DIM
SSD_GROUPS = 4
HEADS_PER_GROUP = SSD_HEADS // SSD_GROUPS
GROUP_WIDTH = HEADS_PER_GROUP * SSD_HEAD_DIM
SSD_STATE = 128
BC_WIDTH = SSD_GROUPS * SSD_STATE
D_CONV = 5
CONV_CH = D_SSD + 2 * BC_WIDTH
CHUNK = 128
D_FNO = 512
FNO_GROUPS = 4
FNO_GROUP_DIM = D_FNO // FNO_GROUPS
N_MEM = 256
MEM_HEADS = 4
D_MEM = 512
MEM_HEAD_DIM = D_MEM // MEM_HEADS
N_BRANCH = 3
EPS = 1e-6
IN_SIZES = (D_SSD, CONV_CH, 2 * SSD_HEADS, D_FNO, D_FNO, D_MEM, D_MEM, N_BRANCH * D_MODEL)
REST_WIDTH = sum(IN_SIZES) - CONV_CH - 2 * SSD_HEADS
XBC_STEP = 256
REST_STEP = REST_WIDTH // (CONV_CH // XBC_STEP)

SUBLANES = 8
HALO_ROWS = 16
CONV_ROWS = 64
CONV_STRIP = 128
CONV_HALO = D_CONV // 2
IN_PROJ_ROWS = 512
MIX_ROWS = 512
W_SPLIT_ROWS = 256
FWD_UNROLL = 8
BWD_UNROLL = 4
VMEM_LIMIT_BYTES = 56 * 1024 * 1024


def _sigmoid(v):
    return 0.5 * jnp.tanh(0.5 * v) + 0.5


def _silu(v):
    return v * _sigmoid(v)


def _softplus(v):
    return jnp.maximum(v, 0.0) + jnp.log1p(jnp.exp(-jnp.abs(v)))


def _rms_scale(v):
    return v * lax.rsqrt(jnp.mean(v * v, axis=-1, keepdims=True) + EPS)


def _dot(a, b):
    return jnp.dot(a, b, preferred_element_type=F32)


def _dot_nt(a, b):
    return lax.dot_general(a, b, (((1,), (1,)), ((), ())), preferred_element_type=F32)


def _dot_exact(a, b):
    return jnp.dot(a, b, precision=HIGHEST, preferred_element_type=F32)


def _in_proj_kernel(tiles_per_seq,
                    x_ref, xprev_ref, xnext_ref, g_ref,
                    wxbc_ref, wrest_ref, wdtT_ref, convw_ref, convb_ref,
                    rest_ref, xsT_ref, bc_ref, dtT_ref,
                    stage_ref, xs_ref, h_ref):
    rows = x_ref.shape[0]
    lt = pl.program_id(0) % tiles_per_seq
    g = g_ref[...]
    keep_prev = (lt > 0).astype(F32)
    keep_next = (lt < tiles_per_seq - 1).astype(F32)
    h_ref[0:HALO_ROWS, :] = (_rms_scale(xprev_ref[...]) * g * keep_prev).astype(BF16)
    h_ref[HALO_ROWS:HALO_ROWS + rows, :] = (_rms_scale(x_ref[...]) * g).astype(BF16)
    h_ref[HALO_ROWS + rows:, :] = (_rms_scale(xnext_ref[...]) * g * keep_next).astype(BF16)
    h_own = h_ref.at[HALO_ROWS:HALO_ROWS + rows, :]

    def project_xbc(j):
        cols = pl.ds(j * XBC_STEP, XBC_STEP)
        stage_ref[:, cols] = _dot(h_ref[...], wxbc_ref[:, cols])

    def project_rest(j):
        cols = pl.ds(j * REST_STEP, REST_STEP)
        rest_ref[:, cols] = _dot(h_own[...], wrest_ref[:, cols]).astype(BF16)

    def conv(j, is_x):
        win_rows = CONV_ROWS + 2 * SUBLANES
        for t in range(XBC_STEP // CONV_STRIP):
            col = j * XBC_STEP + t * CONV_STRIP
            cs = pl.ds(col, CONV_STRIP)
            bias = convb_ref[:, cs]
            taps = [convw_ref[k:k + 1, cs] for k in range(D_CONV)]
            for r0 in range(0, rows, CONV_ROWS):
                lo = HALO_ROWS + r0 - SUBLANES
                win = stage_ref[lo:lo + win_rows, cs]
                acc = bias + taps[CONV_HALO] * win[SUBLANES:SUBLANES + CONV_ROWS]
                for k in range(D_CONV):
                    d = k - CONV_HALO
                    if d != 0:
                        shifted = pltpu.roll(win, (win_rows - d) % win_rows, axis=0)
                        acc = acc + taps[k] * shifted[SUBLANES:SUBLANES + CONV_ROWS]
                act = _silu(acc)
                if is_x:
                    xs_ref[r0:r0 + CONV_ROWS, t * CONV_STRIP:(t + 1) * CONV_STRIP] = act
                else:
                    bc_ref[r0:r0 + CONV_ROWS, pl.ds(col - D_SSD, CONV_STRIP)] = act.astype(BF16)
            if is_x:
                for r0 in range(0, rows, CHUNK):
                    blk = xs_ref[r0:r0 + CHUNK, t * CONV_STRIP:(t + 1) * CONV_STRIP]
                    xsT_ref[0, cs, r0:r0 + CHUNK] = blk.T.astype(BF16)

    project_xbc(0)
    dtT_ref[...] = _dot_nt(wdtT_ref[...], h_own[...])

    n_steps = CONV_CH // XBC_STEP
    for j in range(n_steps):
        conv(j, j * XBC_STEP < D_SSD)
        if j + 1 < n_steps:
            project_xbc(j + 1)
        project_rest(j)


def _const_spec(shape):
    nd = len(shape)
    return pl.BlockSpec(shape, lambda *_: (0,) * nd, pipeline_mode=pl.Buffered(1))


def _layer_spec(stacked, layer):
    return pl.BlockSpec((None,) + stacked.shape[1:], lambda *_: (layer, 0, 0),
                        pipeline_mode=pl.Buffered(1))


def _in_proj(x, norm_g, w_xbc, w_rest, w_dtT, layer, conv_w, conv_b):
    bsz, seq, d = x.shape
    m = bsz * seq
    rows = IN_PROJ_ROWS
    tiles_per_seq = seq // rows
    n_tiles = m // rows
    halo_per_tile = rows // HALO_ROWS
    n_halo = m // HALO_ROWS
    x2 = x.reshape(m, d)

    def row_spec(width):
        return pl.BlockSpec((rows, width), lambda i: (i, 0))

    in_specs = [
        row_spec(d),
        pl.BlockSpec((HALO_ROWS, d), lambda i: (jnp.maximum(i * halo_per_tile - 1, 0), 0)),
        pl.BlockSpec((HALO_ROWS, d),
                     lambda i: (jnp.minimum((i + 1) * halo_per_tile, n_halo - 1), 0)),
        _const_spec((1, d)),
        _layer_spec(w_xbc, layer), _layer_spec(w_rest, layer), _layer_spec(w_dtT, layer),
        _const_spec(conv_w.shape), _const_spec((1, CONV_CH)),
    ]
    out_shape = [
        jax.ShapeDtypeStruct((m, REST_WIDTH), BF16),
        jax.ShapeDtypeStruct((bsz, D_SSD, seq), BF16),
        jax.ShapeDtypeStruct((m, 2 * BC_WIDTH), BF16),
        jax.ShapeDtypeStruct((2 * SSD_HEADS, m), F32),
    ]
    out_specs = [
        row_spec(REST_WIDTH),
        pl.BlockSpec((1, D_SSD, rows), lambda i: (i // tiles_per_seq, 0, i % tiles_per_seq)),
        row_spec(2 * BC_WIDTH),
        pl.BlockSpec((2 * SSD_HEADS, rows), lambda i: (0, i)),
    ]
    return pl.pallas_call(
        functools.partial(_in_proj_kernel, tiles_per_seq),
        grid=(n_tiles,),
        in_specs=in_specs,
        out_specs=out_specs,
        out_shape=out_shape,
        scratch_shapes=[pltpu.VMEM((rows + 2 * HALO_ROWS, CONV_CH), F32),
                        pltpu.VMEM((rows, XBC_STEP), F32),
                        pltpu.VMEM((rows + 2 * HALO_ROWS, d), BF16)],
        compiler_params=pltpu.CompilerParams(
            dimension_semantics=("arbitrary",), vmem_limit_bytes=VMEM_LIMIT_BYTES),
        name="in_proj",
    )(x2, x2, x2, norm_g.reshape(1, d), w_xbc, w_rest, w_dtT, conv_w, conv_b.reshape(1, CONV_CH))


def _expand_rows(v):
    lanes = v.shape[1]
    return jnp.concatenate(
        [jnp.broadcast_to(v[j:j + 1, :], (SSD_HEAD_DIM, lanes)) for j in range(v.shape[0])], axis=0)


TAB_QF, TAB_QB, TAB_EF, TAB_EB, TAB_WF, TAB_WB, TAB_DECAY_F, TAB_DECAY_B, TAB_DTB = range(9)
LOG2E = 1.4426950408889634


def _ssd_kernel(n_chunks,
                xsT_ref, b_ref, c_ref, dtT_ref, bias_col_ref, alog_col_ref, dskip_ref,
                y_ref,
                snap_ref, state_ref, tab_ref, colt_ref):
    H = SSD_HEADS
    row_id = lax.broadcasted_iota(jnp.int32, (CHUNK, CHUNK), 0)
    col_id = lax.broadcasted_iota(jnp.int32, (CHUNK, CHUNK), 1)
    lower = row_id >= col_id
    diag = row_id == col_id
    lower_f = lower.astype(F32)
    upper_f = (row_id <= col_id).astype(F32)

    dt_all = _softplus(dtT_ref[...] + bias_col_ref[...])
    a_all = dt_all * -jnp.exp(alog_col_ref[...])

    def stack(v, lo):
        return jnp.concatenate(
            [v[lo:lo + H, c * CHUNK:(c + 1) * CHUNK] for c in range(n_chunks)], axis=0)

    dt_f, dt_b = stack(dt_all, 0), stack(dt_all, H)
    cum_f = _dot_exact(stack(a_all, 0), upper_f)
    cum_b = _dot_exact(stack(a_all, H), lower_f)
    total_f = jnp.broadcast_to(cum_f[:, CHUNK - 1:CHUNK], cum_f.shape)
    total_b = jnp.broadcast_to(cum_b[:, 0:1], cum_b.shape)
    p_f = cum_f * LOG2E
    p_b = cum_b * LOG2E
    tab_ref[TAB_QF] = p_f - jnp.log2(dt_f)
    tab_ref[TAB_QB] = p_b - jnp.log2(dt_b)
    tab_ref[TAB_EF] = jnp.exp(cum_f)
    tab_ref[TAB_EB] = jnp.exp(cum_b)
    tab_ref[TAB_WF] = dt_f * jnp.exp(total_f - cum_f)
    tab_ref[TAB_WB] = dt_b * jnp.exp(total_b - cum_b)
    tab_ref[TAB_DECAY_F] = jnp.exp(total_f)
    tab_ref[TAB_DECAY_B] = jnp.exp(total_b)
    tab_ref[TAB_DTB] = dt_b
    pad = jnp.zeros((CHUNK - 2 * H, CHUNK), F32)
    for c in range(n_chunks):
        rows = slice(c * H, (c + 1) * H)
        colt_ref[c] = jnp.concatenate([p_f[rows], p_b[rows], pad], axis=0).T

    def chunk_rows(k, c):
        return tab_ref[k, pl.ds(pl.multiple_of(c * H, H), H), :]

    def state_update(c, g, xs_t, w, decay):
        hs = slice(g * HEADS_PER_GROUP, (g + 1) * HEADS_PER_GROUP)
        rs = slice(g * GROUP_WIDTH, (g + 1) * GROUP_WIDTH)
        xd_t = (xs_t * _expand_rows(w[hs])).astype(BF16)
        b_g = b_ref[0, pl.ds(c * CHUNK, CHUNK), g * SSD_STATE:(g + 1) * SSD_STATE]
        state_ref[rs, :] = _expand_rows(decay[hs]) * state_ref[rs, :] + _dot(xd_t, b_g)

    state_ref[...] = jnp.zeros_like(state_ref)

    def fwd_body(c, carry):
        snap_ref[c] = state_ref[...].astype(BF16)
        w, decay = chunk_rows(TAB_WF, c), chunk_rows(TAB_DECAY_F, c)
        for g in range(SSD_GROUPS):
            rs = slice(g * GROUP_WIDTH, (g + 1) * GROUP_WIDTH)
            xs_t = xsT_ref[0, rs, pl.ds(c * CHUNK, CHUNK)].astype(F32)
            state_update(c, g, xs_t, w, decay)
        return carry

    lax.fori_loop(0, n_chunks, fwd_body, 0, unroll=math.gcd(n_chunks, FWD_UNROLL))

    state_ref[...] = jnp.zeros_like(state_ref)

    def bwd_body(it, carry):
        c = n_chunks - 1 - it
        q_f, q_b = chunk_rows(TAB_QF, c), chunk_rows(TAB_QB, c)
        e_f, e_b = chunk_rows(TAB_EF, c), chunk_rows(TAB_EB, c)
        w_b, decay_b = chunk_rows(TAB_WB, c), chunk_rows(TAB_DECAY_B, c)
        dt_b_c = chunk_rows(TAB_DTB, c)
        for g in range(SSD_GROUPS):
            hs = slice(g * HEADS_PER_GROUP, (g + 1) * HEADS_PER_GROUP)
            rs = slice(g * GROUP_WIDTH, (g + 1) * GROUP_WIDTH)
            b_g = b_ref[0, pl.ds(c * CHUNK, CHUNK), g * SSD_STATE:(g + 1) * SSD_STATE]
            c_g = c_ref[0, pl.ds(c * CHUNK, CHUNK), g * SSD_STATE:(g + 1) * SSD_STATE]
            xs_bf = xsT_ref[0, rs, pl.ds(c * CHUNK, CHUNK)]
            xs_t = xs_bf.astype(F32)
            cb = _dot_nt(c_g, b_g)
            y_parts = []
            for j in range(HEADS_PER_GROUP):
                hd = g * HEADS_PER_GROUP + j
                arg = jnp.where(lower,
                                colt_ref[c, :, hd:hd + 1] - q_f[hd:hd + 1, :],
                                colt_ref[c, :, H + hd:H + hd + 1] - q_b[hd:hd + 1, :])
                m_h = (cb * jnp.exp2(arg)).astype(BF16)
                y_parts.append(_dot_nt(xs_bf[j * SSD_HEAD_DIM:(j + 1) * SSD_HEAD_DIM, :], m_h))
            cb_diag = jnp.sum(jnp.where(diag, cb, 0.0), axis=0, keepdims=True)
            own = dskip_ref[rs, :] + _expand_rows(cb_diag * dt_b_c[hs])
            states = jnp.concatenate([snap_ref[c, rs, :], state_ref[rs, :].astype(BF16)], axis=0)
            y_off = _dot_nt(states, c_g)
            y_t = (jnp.concatenate(y_parts, axis=0)
                   + y_off[:GROUP_WIDTH] * _expand_rows(e_f[hs])
                   + y_off[GROUP_WIDTH:] * _expand_rows(e_b[hs])
                   + own * xs_t)
            y_ref[0, pl.ds(c * CHUNK, CHUNK), rs] = y_t.T.astype(BF16)
            state_update(c, g, xs_t, w_b, decay_b)
        return carry

    lax.fori_loop(0, n_chunks, bwd_body, 0, unroll=math.gcd(n_chunks, BWD_UNROLL))


def _ssd(xsT, bc, dtT, dt_bias, a_log, d_skip):
    bsz, _, seq = xsT.shape
    n_chunks = seq // CHUNK
    two_h = 2 * SSD_HEADS
    bias_col = jnp.broadcast_to(dt_bias.reshape(two_h, 1).astype(F32), (two_h, seq))
    alog_col = jnp.broadcast_to(a_log.reshape(two_h, 1).astype(F32), (two_h, seq))
    dskip_col = jnp.broadcast_to(
        jnp.repeat(d_skip.astype(F32), SSD_HEAD_DIM)[:, None], (D_SSD, CHUNK))
    bc3 = bc.reshape(bsz, seq, 2 * BC_WIDTH)
    return pl.pallas_call(
        functools.partial(_ssd_kernel, n_chunks),
        grid=(bsz,),
        in_specs=[
            pl.BlockSpec((1, D_SSD, seq), lambda b: (b, 0, 0)),
            pl.BlockSpec((1, seq, BC_WIDTH), lambda b: (b, 0, 0)),
            pl.BlockSpec((1, seq, BC_WIDTH), lambda b: (b, 0, 1)),
            pl.BlockSpec((two_h, seq), lambda b: (0, b)),
            _const_spec((two_h, seq)), _const_spec((two_h, seq)),
            _const_spec((D_SSD, CHUNK)),
        ],
        out_specs=pl.BlockSpec((1, seq, D_SSD), lambda b: (b, 0, 0)),
        out_shape=jax.ShapeDtypeStruct((bsz, seq, D_SSD), BF16),
        scratch_shapes=[pltpu.VMEM((n_chunks, D_SSD, SSD_STATE), BF16),
                        pltpu.VMEM((D_SSD, SSD_STATE), F32),
                        pltpu.VMEM((TAB_DTB + 1, n_chunks * SSD_HEADS, CHUNK), F32),
                        pltpu.VMEM((n_chunks, CHUNK, CHUNK), F32)],
        compiler_params=pltpu.CompilerParams(
            dimension_semantics=("arbitrary",), vmem_limit_bytes=VMEM_LIMIT_BYTES),
        name="ssd",
    )(xsT, bc3, bc3, dtT, bias_col, alog_col, dskip_col)


def _mix_kernel(apply_final_norm,
                x_ref, ys_ref, zs_ref, ssdg_ref, uf_ref, zf_ref, q_ref, zm_ref, gates_ref,
                mem_ref, memg_ref, wkv_ref, chan_dft_ref, seq_dft_ref,
                pssd_ref, pfno_ref, pmem_ref, wout_ref, finalg_ref,
                out_ref,
                ab_ref, kv_ref):
    seq = uf_ref.shape[1]
    lt = pl.program_id(1)

    @pl.when(lt == 0)
    def _per_sequence():
        for gi in range(FNO_GROUPS):
            cs = slice(gi * FNO_GROUP_DIM, (gi + 1) * FNO_GROUP_DIM)
            t = _dot(uf_ref[0, :, cs], chan_dft_ref[...])
            ab_ref[0:seq, cs] = t[:, :FNO_GROUP_DIM].astype(BF16)
            ab_ref[seq:2 * seq, cs] = t[:, FNO_GROUP_DIM:].astype(BF16)
        hm = (_rms_scale(mem_ref[0]) * memg_ref[...]).astype(BF16)
        kv_ref[...] = _dot(hm, wkv_ref[...]).astype(BF16)

    t = ys_ref[0].astype(F32) * _silu(zs_ref[0].astype(F32))
    parts = []
    for gi in range(SSD_GROUPS):
        parts.append(_rms_scale(t[:, gi * GROUP_WIDTH:(gi + 1) * GROUP_WIDTH]))
    ysn = (jnp.concatenate(parts, axis=1) * ssdg_ref[...]).astype(BF16)
    m_s = _dot(ysn, pssd_ref[...])

    yf = _dot(seq_dft_ref[...], ab_ref[...]) * _silu(zf_ref[0].astype(F32))
    m_f = _dot(yf.astype(BF16), pfno_ref[...])

    q = q_ref[0]
    heads = []
    for hd in range(MEM_HEADS):
        cs = slice(hd * MEM_HEAD_DIM, (hd + 1) * MEM_HEAD_DIM)
        s = _dot_nt(q[:, cs], kv_ref[:, cs]) * (MEM_HEAD_DIM ** -0.5)
        p = jnp.exp(s - jnp.max(s, axis=-1, keepdims=True))
        denom = jnp.sum(p, axis=-1, keepdims=True)
        pv = _dot(p.astype(BF16), kv_ref[:, D_MEM + hd * MEM_HEAD_DIM:D_MEM + (hd + 1) * MEM_HEAD_DIM])
        heads.append(pv / denom)
    ym = jnp.concatenate(heads, axis=1) * _silu(zm_ref[0].astype(F32))
    m_m = _dot(ym.astype(BF16), pmem_ref[...])

    gate = _sigmoid(gates_ref[0].astype(F32))
    merged = (gate[:, :D_MODEL] * m_s + gate[:, D_MODEL:2 * D_MODEL] * m_f
              + gate[:, 2 * D_MODEL:] * m_m)
    out = x_ref[0] + _dot(merged.astype(BF16), wout_ref[...])
    if apply_final_norm:
        out = _rms_scale(out) * finalg_ref[...]
    out_ref[0] = out


def _mix(x, ys, rest, ssd_norm_g, mem, mem_norm_g, chan_dft, seq_dft, w_kv, p_ssd, p_fno, p_mem,
         w_out, layer, final_g, apply_final_norm):
    bsz, seq, d = x.shape
    rows = MIX_ROWS
    n_tiles = seq // rows
    rest3 = rest.reshape(bsz, seq, REST_WIDTH)

    def tile_spec(width, col_block=0):
        return pl.BlockSpec((1, rows, width), lambda b, t: (b, t, col_block))

    assert D_SSD == 2 * D_FNO and D_FNO == D_MEM and REST_WIDTH == 2 * N_BRANCH * D_MODEL
    in_specs = [
        tile_spec(d), tile_spec(D_SSD), tile_spec(D_SSD, 0), _const_spec((1, D_SSD)),
        pl.BlockSpec((1, seq, D_FNO), lambda b, t: (b, 0, 2)),
        tile_spec(D_FNO, 3), tile_spec(D_MEM, 4), tile_spec(D_MEM, 5),
        tile_spec(N_BRANCH * D_MODEL, 1),
        pl.BlockSpec((1, N_MEM, d), lambda b, t: (b, 0, 0)),
        _const_spec((1, d)), _layer_spec(w_kv, layer), _const_spec(chan_dft.shape),
        pl.BlockSpec((rows, 2 * seq), lambda b, t: (t, 0)),
        _layer_spec(p_ssd, layer), _layer_spec(p_fno, layer), _layer_spec(p_mem, layer),
        _layer_spec(w_out, layer), _const_spec((1, d)),
    ]
    return pl.pallas_call(
        functools.partial(_mix_kernel, apply_final_norm),
        grid=(bsz, n_tiles),
        in_specs=in_specs,
        out_specs=tile_spec(d),
        out_shape=jax.ShapeDtypeStruct((bsz, seq, d), F32),
        scratch_shapes=[pltpu.VMEM((2 * seq, D_FNO), BF16),
                        pltpu.VMEM((N_MEM, 2 * D_MEM), BF16)],
        compiler_params=pltpu.CompilerParams(
            dimension_semantics=("arbitrary", "arbitrary"), vmem_limit_bytes=VMEM_LIMIT_BYTES),
        name="mix",
    )(x, ys, rest3, ssd_norm_g.reshape(1, D_SSD), rest3, rest3, rest3, rest3, rest3, mem,
      mem_norm_g.reshape(1, d), w_kv, chan_dft, seq_dft, p_ssd, p_fno, p_mem, w_out,
      final_g.reshape(1, d))


def _dft_table(n, split, sin_sign):
    m = jnp.arange(n, dtype=jnp.int32)
    k_hi = jnp.arange(n // split, dtype=jnp.int32) * split
    k_lo = jnp.arange(split, dtype=jnp.int32)
    ang_hi = (2.0 * math.pi / n) * ((k_hi[:, None] * m[None, :]) % n).astype(F32)
    ang_lo = (2.0 * math.pi / n) * ((k_lo[:, None] * m[None, :]) % n).astype(F32)
    ca, sa = jnp.cos(ang_hi), jnp.sin(ang_hi)
    cb, sb = jnp.cos(ang_lo)[None, :, None, :], jnp.sin(ang_lo)[None, :, None, :]
    u = jnp.stack([ca, sin_sign * sa], axis=1)[:, None, :, :]
    v = jnp.stack([-sa, sin_sign * ca], axis=1)[:, None, :, :]
    return ((u * cb + v * sb) * (n ** -0.5)).astype(BF16).reshape(n, 2 * n)


def _dft_tables(seq):
    return _dft_table(FNO_GROUP_DIM, 8, 1.0), _dft_table(seq, 64, -1.0)


def _split_w_in_kernel(w_ref, xbc_ref, rest_ref, dt_ref):
    xbc_lo, dt_lo = IN_SIZES[0], IN_SIZES[0] + IN_SIZES[1]
    rest_lo = dt_lo + IN_SIZES[2]
    xbc_ref[0] = w_ref[0, :, xbc_lo:dt_lo].astype(BF16)
    dt_ref[0] = w_ref[0, :, dt_lo:rest_lo].astype(BF16)
    rest_ref[0, :, :xbc_lo] = w_ref[0, :, :xbc_lo].astype(BF16)
    rest_ref[0, :, xbc_lo:] = w_ref[0, :, rest_lo:].astype(BF16)


def _split_w_in(w_in):
    depth, d, d_in = w_in.shape
    rows = W_SPLIT_ROWS

    def spec(width):
        return pl.BlockSpec((1, rows, width), lambda i, r: (i, r, 0))

    return pl.pallas_call(
        _split_w_in_kernel,
        grid=(depth, d // rows),
        in_specs=[spec(d_in)],
        out_specs=[spec(CONV_CH), spec(REST_WIDTH), spec(2 * SSD_HEADS)],
        out_shape=[jax.ShapeDtypeStruct((depth, d, CONV_CH), BF16),
                   jax.ShapeDtypeStruct((depth, d, REST_WIDTH), BF16),
                   jax.ShapeDtypeStruct((depth, d, 2 * SSD_HEADS), BF16)],
        compiler_params=pltpu.CompilerParams(
            dimension_semantics=("arbitrary", "arbitrary"), vmem_limit_bytes=VMEM_LIMIT_BYTES),
        name="split_w_in",
    )(w_in)


def _layer(x, mem, layer, norm_g, w_in_parts, conv_w, conv_b, dt_bias, a_log, d_skip, ssd_norm_g,
           mem_norm_g, mix_weights, final_g, tables, apply_final_norm):
    w_xbc, w_rest, w_dtT = w_in_parts
    rest, xsT, bc, dtT = _in_proj(x, norm_g, w_xbc, w_rest, w_dtT, layer, conv_w, conv_b)
    ys = _ssd(xsT, bc, dtT, dt_bias, a_log, d_skip)
    return _mix(x, ys, rest, ssd_norm_g, mem, mem_norm_g, *tables, *mix_weights, layer, final_g,
                apply_final_norm)


def kernel(x, mem, norm_g, w_in, conv_w, conv_b, dt_bias, a_log, d_skip, ssd_norm_g, mem_norm_g,
           w_kv, p_ssd, p_fno, p_mem, w_out, final_g):
    depth = norm_g.shape[0]
    tables = _dft_tables(x.shape[1])
    w_xbc, w_rest, w_dt = _split_w_in(w_in)
    w_in_parts = (w_xbc, w_rest, jnp.swapaxes(w_dt, 1, 2))
    mix_weights = tuple(w.astype(BF16) for w in (w_kv, p_ssd, p_fno, p_mem, w_out))
    for i in range(depth):
        x = _layer(x, mem, i, norm_g[i], w_in_parts, conv_w[i], conv_b[i], dt_bias[i], a_log[i],
                   d_skip[i], ssd_norm_g[i], mem_norm_g[i], mix_weights, final_g, tables,
                   i == depth - 1)
    return x
```

```python
import functools
import math

import jax
import jax.numpy as jnp
from jax import lax
from jax.experimental import pallas as pl
from jax.experimental.pallas import tpu as pltpu

F32 = jnp.float32
BF16 = jnp.bfloat16
HIGHEST = lax.Precision.HIGHEST

D_MODEL = 1024
D_SSD = 1024
SSD_HEAD_DIM = 64
SSD_HEADS = D_SSD // SSD_HEAD_DIM
SSD_GROUPS = 4
HEADS_PER_GROUP = SSD_HEADS // SSD_GROUPS
GROUP_WIDTH = HEADS_PER_GROUP * SSD_HEAD_DIM
SSD_STATE = 128
BC_WIDTH = SSD_GROUPS * SSD_STATE
D_CONV = 5
CONV_CH = D_SSD + 2 * BC_WIDTH
CHUNK = 128
D_FNO = 512
FNO_GROUPS = 4
FNO_GROUP_DIM = D_FNO // FNO_GROUPS
N_MEM = 256
MEM_HEADS = 4
D_MEM = 512
MEM_HEAD_DIM = D_MEM // MEM_HEADS
N_BRANCH = 3
EPS = 1e-6
IN_SIZES = (D_SSD, CONV_CH, 2 * SSD_HEADS, D_FNO, D_FNO, D_MEM, D_MEM, N_BRANCH * D_MODEL)
REST_WIDTH = sum(IN_SIZES) - CONV_CH - 2 * SSD_HEADS
XBC_STEP = 256
REST_STEP = REST_WIDTH // (CONV_CH // XBC_STEP)

SUBLANES = 8
HALO_ROWS = 16
CONV_ROWS = 64
CONV_STRIP = 128
CONV_HALO = D_CONV // 2
IN_PROJ_ROWS = 512
MIX_ROWS = 512
W_SPLIT_COLS = 1024
FWD_UNROLL = 8
BWD_UNROLL = 4
VMEM_LIMIT_BYTES = 56 * 1024 * 1024


def _sigmoid(v):
    return 0.5 * jnp.tanh(0.5 * v) + 0.5


def _silu(v):
    return v * _sigmoid(v)


def _softplus(v):
    return jnp.maximum(v, 0.0) + jnp.log1p(jnp.exp(-jnp.abs(v)))


def _rms_scale(v):
    return v * lax.rsqrt(jnp.mean(v * v, axis=-1, keepdims=True) + EPS)


def _dot(a, b):
    return jnp.dot(a, b, preferred_element_type=F32)


def _dot_nt(a, b):
    return lax.dot_general(a, b, (((1,), (1,)), ((), ())), preferred_element_type=F32)


def _dot_exact(a, b):
    return jnp.dot(a, b, precision=HIGHEST, preferred_element_type=F32)


def _in_proj_kernel(tiles_per_seq,
                    x_ref, xprev_ref, xnext_ref, g_ref,
                    wxbc_ref, wrest_ref, wdtT_ref, convw_ref, convb_ref,
                    rest_ref, xsT_ref, bc_ref, dtT_ref,
                    stage_ref, xs_ref, h_ref):
    rows = x_ref.shape[0]
    lt = pl.program_id(0) % tiles_per_seq
    g = g_ref[...]
    keep_prev = (lt > 0).astype(F32)
    keep_next = (lt < tiles_per_seq - 1).astype(F32)
    h_ref[0:HALO_ROWS, :] = (_rms_scale(xprev_ref[...]) * g * keep_prev).astype(BF16)
    h_ref[HALO_ROWS:HALO_ROWS + rows, :] = (_rms_scale(x_ref[...]) * g).astype(BF16)
    h_ref[HALO_ROWS + rows:, :] = (_rms_scale(xnext_ref[...]) * g * keep_next).astype(BF16)
    h_own = h_ref.at[HALO_ROWS:HALO_ROWS + rows, :]

    def project_xbc(j):
        cols = pl.ds(j * XBC_STEP, XBC_STEP)
        stage_ref[:, cols] = _dot(h_ref[...], wxbc_ref[:, cols])

    def project_rest(j):
        cols = pl.ds(j * REST_STEP, REST_STEP)
        rest_ref[:, cols] = _dot(h_own[...], wrest_ref[:, cols]).astype(BF16)

    def conv(j, is_x):
        win_rows = CONV_ROWS + 2 * SUBLANES
        for t in range(XBC_STEP // CONV_STRIP):
            col = j * XBC_STEP + t * CONV_STRIP
            cs = pl.ds(col, CONV_STRIP)
            bias = convb_ref[:, cs]
            taps = [convw_ref[k:k + 1, cs] for k in range(D_CONV)]
            for r0 in range(0, rows, CONV_ROWS):
                lo = HALO_ROWS + r0 - SUBLANES
                win = stage_ref[lo:lo + win_rows, cs]
                acc = bias + taps[CONV_HALO] * win[SUBLANES:SUBLANES + CONV_ROWS]
                for k in range(D_CONV):
                    d = k - CONV_HALO
                    if d != 0:
                        shifted = pltpu.roll(win, (win_rows - d) % win_rows, axis=0)
                        acc = acc + taps[k] * shifted[SUBLANES:SUBLANES + CONV_ROWS]
                act = _silu(acc)
                if is_x:
                    xs_ref[r0:r0 + CONV_ROWS, t * CONV_STRIP:(t + 1) * CONV_STRIP] = act
                else:
                    bc_ref[r0:r0 + CONV_ROWS, pl.ds(col - D_SSD, CONV_STRIP)] = act.astype(BF16)
            if is_x:
                for r0 in range(0, rows, CHUNK):
                    blk = xs_ref[r0:r0 + CHUNK, t * CONV_STRIP:(t + 1) * CONV_STRIP]
                    xsT_ref[0, cs, r0:r0 + CHUNK] = blk.T.astype(BF16)

    project_xbc(0)
    dtT_ref[...] = _dot_nt(wdtT_ref[...], h_own[...])

    n_steps = CONV_CH // XBC_STEP
    for j in range(n_steps):
        conv(j, j * XBC_STEP < D_SSD)
        if j + 1 < n_steps:
            project_xbc(j + 1)
        project_rest(j)


def _const_spec(shape):
    nd = len(shape)
    return pl.BlockSpec(shape, lambda *_: (0,) * nd, pipeline_mode=pl.Buffered(1))


def _layer_spec(stacked, layer):
    return pl.BlockSpec((None,) + stacked.shape[1:], lambda *_: (layer, 0, 0),
                        pipeline_mode=pl.Buffered(1))


def _in_proj(x, norm_g, w_xbc, w_rest, w_dtT, layer, conv_w, conv_b):
    bsz, seq, d = x.shape
    m = bsz * seq
    rows = IN_PROJ_ROWS
    tiles_per_seq = seq // rows
    n_tiles = m // rows
    halo_per_tile = rows // HALO_ROWS
    n_halo = m // HALO_ROWS
    x2 = x.reshape(m, d)

    def row_spec(width):
        return pl.BlockSpec((rows, width), lambda i: (i, 0))

    in_specs = [
        row_spec(d),
        pl.BlockSpec((HALO_ROWS, d), lambda i: (jnp.maximum(i * halo_per_tile - 1, 0), 0)),
        pl.BlockSpec((HALO_ROWS, d),
                     lambda i: (jnp.minimum((i + 1) * halo_per_tile, n_halo - 1), 0)),
        _const_spec((1, d)),
        _layer_spec(w_xbc, layer), _layer_spec(w_rest, layer), _layer_spec(w_dtT, layer),
        _const_spec(conv_w.shape), _const_spec((1, CONV_CH)),
    ]
    out_shape = [
        jax.ShapeDtypeStruct((m, REST_WIDTH), BF16),
        jax.ShapeDtypeStruct((bsz, D_SSD, seq), BF16),
        jax.ShapeDtypeStruct((m, 2 * BC_WIDTH), BF16),
        jax.ShapeDtypeStruct((2 * SSD_HEADS, m), F32),
    ]
    out_specs = [
        row_spec(REST_WIDTH),
        pl.BlockSpec((1, D_SSD, rows), lambda i: (i // tiles_per_seq, 0, i % tiles_per_seq)),
        row_spec(2 * BC_WIDTH),
        pl.BlockSpec((2 * SSD_HEADS, rows), lambda i: (0, i)),
    ]
    return pl.pallas_call(
        functools.partial(_in_proj_kernel, tiles_per_seq),
        grid=(n_tiles,),
        in_specs=in_specs,
        out_specs=out_specs,
        out_shape=out_shape,
        scratch_shapes=[pltpu.VMEM((rows + 2 * HALO_ROWS, CONV_CH), F32),
                        pltpu.VMEM((rows, XBC_STEP), F32),
                        pltpu.VMEM((rows + 2 * HALO_ROWS, d), BF16)],
        compiler_params=pltpu.CompilerParams(
            dimension_semantics=("arbitrary",), vmem_limit_bytes=VMEM_LIMIT_BYTES),
        name="in_proj",
    )(x2, x2, x2, norm_g.reshape(1, d), w_xbc, w_rest, w_dtT, conv_w, conv_b.reshape(1, CONV_CH))


def _expand_rows(v):
    lanes = v.shape[1]
    return jnp.concatenate(
        [jnp.broadcast_to(v[j:j + 1, :], (SSD_HEAD_DIM, lanes)) for j in range(v.shape[0])], axis=0)


TAB_QF, TAB_QB, TAB_EF, TAB_EB, TAB_WF, TAB_WB, TAB_DECAY_F, TAB_DECAY_B, TAB_DTB = range(9)
LOG2E = 1.4426950408889634


def _ssd_kernel(n_chunks,
                xsT_ref, b_ref, c_ref, dtT_ref, bias_col_ref, alog_col_ref, dskip_ref,
                y_ref,
                snap_ref, state_ref, tab_ref, colt_ref):
    H = SSD_HEADS
    row_id = lax.broadcasted_iota(jnp.int32, (CHUNK, CHUNK), 0)
    col_id = lax.broadcasted_iota(jnp.int32, (CHUNK, CHUNK), 1)
    lower = row_id >= col_id
    diag = row_id == col_id
    lower_f = lower.astype(F32)
    upper_f = (row_id <= col_id).astype(F32)

    dt_all = _softplus(dtT_ref[...] + bias_col_ref[...])
    a_all = dt_all * -jnp.exp(alog_col_ref[...])

    def stack(v, lo):
        return jnp.concatenate(
            [v[lo:lo + H, c * CHUNK:(c + 1) * CHUNK] for c in range(n_chunks)], axis=0)

    dt_f, dt_b = stack(dt_all, 0), stack(dt_all, H)
    cum_f = _dot_exact(stack(a_all, 0), upper_f)
    cum_b = _dot_exact(stack(a_all, H), lower_f)
    total_f = jnp.broadcast_to(cum_f[:, CHUNK - 1:CHUNK], cum_f.shape)
    total_b = jnp.broadcast_to(cum_b[:, 0:1], cum_b.shape)
    p_f = cum_f * LOG2E
    p_b = cum_b * LOG2E
    tab_ref[TAB_QF] = p_f - jnp.log2(dt_f)
    tab_ref[TAB_QB] = p_b - jnp.log2(dt_b)
    tab_ref[TAB_EF] = jnp.exp(cum_f)
    tab_ref[TAB_EB] = jnp.exp(cum_b)
    tab_ref[TAB_WF] = dt_f * jnp.exp(total_f - cum_f)
    tab_ref[TAB_WB] = dt_b * jnp.exp(total_b - cum_b)
    tab_ref[TAB_DECAY_F] = jnp.exp(total_f)
    tab_ref[TAB_DECAY_B] = jnp.exp(total_b)
    tab_ref[TAB_DTB] = dt_b
    pad = jnp.zeros((CHUNK - 2 * H, CHUNK), F32)
    for c in range(n_chunks):
        rows = slice(c * H, (c + 1) * H)
        colt_ref[c] = jnp.concatenate([p_f[rows], p_b[rows], pad], axis=0).T

    def chunk_rows(k, c):
        return tab_ref[k, pl.ds(pl.multiple_of(c * H, H), H), :]

    def state_update(c, g, xs_t, w, decay):
        hs = slice(g * HEADS_PER_GROUP, (g + 1) * HEADS_PER_GROUP)
        rs = slice(g * GROUP_WIDTH, (g + 1) * GROUP_WIDTH)
        xd_t = (xs_t * _expand_rows(w[hs])).astype(BF16)
        b_g = b_ref[0, pl.ds(c * CHUNK, CHUNK), g * SSD_STATE:(g + 1) * SSD_STATE]
        state_ref[rs, :] = _expand_rows(decay[hs]) * state_ref[rs, :] + _dot(xd_t, b_g)

    state_ref[...] = jnp.zeros_like(state_ref)

    def fwd_body(c, carry):
        snap_ref[c] = state_ref[...].astype(BF16)
        w, decay = chunk_rows(TAB_WF, c), chunk_rows(TAB_DECAY_F, c)
        for g in range(SSD_GROUPS):
            rs = slice(g * GROUP_WIDTH, (g + 1) * GROUP_WIDTH)
            xs_t = xsT_ref[0, rs, pl.ds(c * CHUNK, CHUNK)].astype(F32)
            state_update(c, g, xs_t, w, decay)
        return carry

    lax.fori_loop(0, n_chunks, fwd_body, 0, unroll=math.gcd(n_chunks, FWD_UNROLL))

    state_ref[...] = jnp.zeros_like(state_ref)

    def bwd_body(it, carry):
        c = n_chunks - 1 - it
        q_f, q_b = chunk_rows(TAB_QF, c), chunk_rows(TAB_QB, c)
        e_f, e_b = chunk_rows(TAB_EF, c), chunk_rows(TAB_EB, c)
        w_b, decay_b = chunk_rows(TAB_WB, c), chunk_rows(TAB_DECAY_B, c)
        dt_b_c = chunk_rows(TAB_DTB, c)
        for g in range(SSD_GROUPS):
            hs = slice(g * HEADS_PER_GROUP, (g + 1) * HEADS_PER_GROUP)
            rs = slice(g * GROUP_WIDTH, (g + 1) * GROUP_WIDTH)
            b_g = b_ref[0, pl.ds(c * CHUNK, CHUNK), g * SSD_STATE:(g + 1) * SSD_STATE]
            c_g = c_ref[0, pl.ds(c * CHUNK, CHUNK), g * SSD_STATE:(g + 1) * SSD_STATE]
            xs_bf = xsT_ref[0, rs, pl.ds(c * CHUNK, CHUNK)]
            xs_t = xs_bf.astype(F32)
            cb = _dot_nt(c_g, b_g)
            y_parts = []
            for j in range(HEADS_PER_GROUP):
                hd = g * HEADS_PER_GROUP + j
                arg = jnp.where(lower,
                                colt_ref[c, :, hd:hd + 1] - q_f[hd:hd + 1, :],
                                colt_ref[c, :, H + hd:H + hd + 1] - q_b[hd:hd + 1, :])
                m_h = (cb * jnp.exp2(arg)).astype(BF16)
                y_parts.append(_dot_nt(xs_bf[j * SSD_HEAD_DIM:(j + 1) * SSD_HEAD_DIM, :], m_h))
            cb_diag = jnp.sum(jnp.where(diag, cb, 0.0), axis=0, keepdims=True)
            own = dskip_ref[rs, :] + _expand_rows(cb_diag * dt_b_c[hs])
            states = jnp.concatenate([snap_ref[c, rs, :], state_ref[rs, :].astype(BF16)], axis=0)
            y_off = _dot_nt(states, c_g)
            y_t = (jnp.concatenate(y_parts, axis=0)
                   + y_off[:GROUP_WIDTH] * _expand_rows(e_f[hs])
                   + y_off[GROUP_WIDTH:] * _expand_rows(e_b[hs])
                   + own * xs_t)
            y_ref[0, pl.ds(c * CHUNK, CHUNK), rs] = y_t.T.astype(BF16)
            state_update(c, g, xs_t, w_b, decay_b)
        return carry

    lax.fori_loop(0, n_chunks, bwd_body, 0, unroll=math.gcd(n_chunks, BWD_UNROLL))


def _ssd(xsT, bc, dtT, dt_bias, a_log, d_skip):
    bsz, _, seq = xsT.shape
    n_chunks = seq // CHUNK
    two_h = 2 * SSD_HEADS
    bias_col = jnp.broadcast_to(dt_bias.reshape(two_h, 1).astype(F32), (two_h, seq))
    alog_col = jnp.broadcast_to(a_log.reshape(two_h, 1).astype(F32), (two_h, seq))
    dskip_col = jnp.broadcast_to(
        jnp.repeat(d_skip.astype(F32), SSD_HEAD_DIM)[:, None], (D_SSD, CHUNK))
    bc3 = bc.reshape(bsz, seq, 2 * BC_WIDTH)
    return pl.pallas_call(
        functools.partial(_ssd_kernel, n_chunks),
        grid=(bsz,),
        in_specs=[
            pl.BlockSpec((1, D_SSD, seq), lambda b: (b, 0, 0)),
            pl.BlockSpec((1, seq, BC_WIDTH), lambda b: (b, 0, 0)),
            pl.BlockSpec((1, seq, BC_WIDTH), lambda b: (b, 0, 1)),
            pl.BlockSpec((two_h, seq), lambda b: (0, b)),
            _const_spec((two_h, seq)), _const_spec((two_h, seq)),
            _const_spec((D_SSD, CHUNK)),
        ],
        out_specs=pl.BlockSpec((1, seq, D_SSD), lambda b: (b, 0, 0)),
        out_shape=jax.ShapeDtypeStruct((bsz, seq, D_SSD), BF16),
        scratch_shapes=[pltpu.VMEM((n_chunks, D_SSD, SSD_STATE), BF16),
                        pltpu.VMEM((D_SSD, SSD_STATE), F32),
                        pltpu.VMEM((TAB_DTB + 1, n_chunks * SSD_HEADS, CHUNK), F32),
                        pltpu.VMEM((n_chunks, CHUNK, CHUNK), F32)],
        compiler_params=pltpu.CompilerParams(
            dimension_semantics=("arbitrary",), vmem_limit_bytes=VMEM_LIMIT_BYTES),
        name="ssd",
    )(xsT, bc3, bc3, dtT, bias_col, alog_col, dskip_col)


def _mix_kernel(apply_final_norm,
                x_ref, ys_ref, zs_ref, ssdg_ref, uf_ref, zf_ref, q_ref, zm_ref, gates_ref,
                mem_ref, memg_ref, wkv_ref, chan_dft_ref, seq_dft_ref,
                pssd_ref, pfno_ref, pmem_ref, wout_ref, finalg_ref,
                out_ref,
                ab_ref, kv_ref, dft_ref, mid_ref):
    seq = uf_ref.shape[1]
    lt = pl.program_id(1)

    @pl.when(lt == 0)
    def _per_sequence():
        half = seq // 2
        r_id = lax.broadcasted_iota(jnp.int32, (CHUNK, 2 * CHUNK), 0)
        c_id = lax.broadcasted_iota(jnp.int32, (CHUNK, 2 * CHUNK), 1)
        reverse = (r_id + c_id == CHUNK).astype(BF16)
        sign = jnp.concatenate([jnp.ones((1, D_FNO), F32), -jnp.ones((1, D_FNO), F32)], axis=1)
        for gi in range(FNO_GROUPS):
            cs = slice(gi * FNO_GROUP_DIM, (gi + 1) * FNO_GROUP_DIM)
            t = _dot(uf_ref[0, :, cs], chan_dft_ref[...]).astype(BF16)
            dft_ref[:, cs] = t[:, :FNO_GROUP_DIM]
            dft_ref[:, D_FNO + gi * FNO_GROUP_DIM:D_FNO + (gi + 1) * FNO_GROUP_DIM] = (
                t[:, FNO_GROUP_DIM:])
        mid_ref[0:1, :] = dft_ref[half:half + 1, :D_FNO].astype(F32)
        for blk in range(half // CHUNK):
            lo = seq - (blk + 1) * CHUNK
            if blk == 0:
                window = jnp.concatenate(
                    [dft_ref[lo:, :], jnp.zeros((CHUNK, 2 * D_FNO), BF16)], axis=0)
            else:
                window = dft_ref[lo:lo + 2 * CHUNK, :]
            own = dft_ref[blk * CHUNK:(blk + 1) * CHUNK, :].astype(F32)
            folded = (own + sign * _dot(reverse, window)).astype(BF16)
            ab_ref[blk * CHUNK:(blk + 1) * CHUNK, :] = folded[:, :D_FNO]
            ab_ref[half + blk * CHUNK:half + (blk + 1) * CHUNK, :] = folded[:, D_FNO:]
        hm = (_rms_scale(mem_ref[0]) * memg_ref[...]).astype(BF16)
        kv_ref[...] = _dot(hm, wkv_ref[...]).astype(BF16)

    t = ys_ref[0].astype(F32) * _silu(zs_ref[0].astype(F32))
    parts = []
    for gi in range(SSD_GROUPS):
        parts.append(_rms_scale(t[:, gi * GROUP_WIDTH:(gi + 1) * GROUP_WIDTH]))
    ysn = (jnp.concatenate(parts, axis=1) * ssdg_ref[...]).astype(BF16)
    m_s = _dot(ysn, pssd_ref[...])

    k_id = lax.broadcasted_iota(jnp.int32, (x_ref.shape[1], D_FNO), 0)
    alt = jnp.where(k_id % 2 == 0, seq ** -0.5, -(seq ** -0.5))
    yf = _dot(seq_dft_ref[...], ab_ref[...]) + alt * mid_ref[0:1, :]
    yf = yf * _silu(zf_ref[0].astype(F32))
    m_f = _dot(yf.astype(BF16), pfno_ref[...])

    q = q_ref[0]
    heads = []
    for hd in range(MEM_HEADS):
        cs = slice(hd * MEM_HEAD_DIM, (hd + 1) * MEM_HEAD_DIM)
        s = _dot_nt(q[:, cs], kv_ref[:, cs]) * (MEM_HEAD_DIM ** -0.5)
        p = jnp.exp(s - jnp.max(s, axis=-1, keepdims=True))
        denom = jnp.sum(p, axis=-1, keepdims=True)
        pv = _dot(p.astype(BF16), kv_ref[:, D_MEM + hd * MEM_HEAD_DIM:D_MEM + (hd + 1) * MEM_HEAD_DIM])
        heads.append(pv / denom)
    ym = jnp.concatenate(heads, axis=1) * _silu(zm_ref[0].astype(F32))
    m_m = _dot(ym.astype(BF16), pmem_ref[...])

    gate = _sigmoid(gates_ref[0].astype(F32))
    merged = (gate[:, :D_MODEL] * m_s + gate[:, D_MODEL:2 * D_MODEL] * m_f
              + gate[:, 2 * D_MODEL:] * m_m)
    out = x_ref[0] + _dot(merged.astype(BF16), wout_ref[...])
    if apply_final_norm:
        out = _rms_scale(out) * finalg_ref[...]
    out_ref[0] = out


def _mix(x, ys, rest, ssd_norm_g, mem, mem_norm_g, chan_dft, seq_dft, w_kv, p_ssd, p_fno, p_mem,
         w_out, layer, final_g, apply_final_norm):
    bsz, seq, d = x.shape
    rows = MIX_ROWS
    n_tiles = seq // rows
    rest3 = rest.reshape(bsz, seq, REST_WIDTH)

    def tile_spec(width, col_block=0):
        return pl.BlockSpec((1, rows, width), lambda b, t: (b, t, col_block))

    assert D_SSD == 2 * D_FNO and D_FNO == D_MEM and REST_WIDTH == 2 * N_BRANCH * D_MODEL
    in_specs = [
        tile_spec(d), tile_spec(D_SSD), tile_spec(D_SSD, 0), _const_spec((1, D_SSD)),
        pl.BlockSpec((1, seq, D_FNO), lambda b, t: (b, 0, 2)),
        tile_spec(D_FNO, 3), tile_spec(D_MEM, 4), tile_spec(D_MEM, 5),
        tile_spec(N_BRANCH * D_MODEL, 1),
        pl.BlockSpec((1, N_MEM, d), lambda b, t: (b, 0, 0)),
        _const_spec((1, d)), _layer_spec(w_kv, layer), _const_spec(chan_dft.shape),
        pl.BlockSpec((rows, seq), lambda b, t: (t, 0)),
        _layer_spec(p_ssd, layer), _layer_spec(p_fno, layer), _layer_spec(p_mem, layer),
        _layer_spec(w_out, layer), _const_spec((1, d)),
    ]
    return pl.pallas_call(
        functools.partial(_mix_kernel, apply_final_norm),
        grid=(bsz, n_tiles),
        in_specs=in_specs,
        out_specs=tile_spec(d),
        out_shape=jax.ShapeDtypeStruct((bsz, seq, d), F32),
        scratch_shapes=[pltpu.VMEM((seq, D_FNO), BF16),
                        pltpu.VMEM((N_MEM, 2 * D_MEM), BF16),
                        pltpu.VMEM((seq, 2 * D_FNO), BF16),
                        pltpu.VMEM((SUBLANES, D_FNO), F32)],
        compiler_params=pltpu.CompilerParams(
            dimension_semantics=("arbitrary", "arbitrary"), vmem_limit_bytes=VMEM_LIMIT_BYTES),
        name="mix",
    )(x, ys, rest3, ssd_norm_g.reshape(1, D_SSD), rest3, rest3, rest3, rest3, rest3, mem,
      mem_norm_g.reshape(1, d), w_kv, chan_dft, seq_dft, p_ssd, p_fno, p_mem, w_out,
      final_g.reshape(1, d))


def _dft_table(n, split, sin_sign, n_cols):
    m = jnp.arange(n_cols, dtype=jnp.int32)
    k_hi = jnp.arange(n // split, dtype=jnp.int32) * split
    k_lo = jnp.arange(split, dtype=jnp.int32)
    ang_hi = (2.0 * math.pi / n) * ((k_hi[:, None] * m[None, :]) % n).astype(F32)
    ang_lo = (2.0 * math.pi / n) * ((k_lo[:, None] * m[None, :]) % n).astype(F32)
    ca, sa = jnp.cos(ang_hi), jnp.sin(ang_hi)
    cb, sb = jnp.cos(ang_lo), jnp.sin(ang_lo)
    u = jnp.concatenate([ca, sin_sign * sa], axis=1)[:, None, :]
    v = jnp.concatenate([-sa, sin_sign * ca], axis=1)[:, None, :]
    cb2 = jnp.concatenate([cb, cb], axis=1)[None, :, :]
    sb2 = jnp.concatenate([sb, sb], axis=1)[None, :, :]
    return ((u * cb2 + v * sb2) * (n ** -0.5)).astype(BF16).reshape(n, 2 * n_cols)


def _dft_tables(seq):
    return (_dft_table(FNO_GROUP_DIM, 8, 1.0, FNO_GROUP_DIM),
            _dft_table(seq, 64, -1.0, seq // 2))


def _transpose_cast_kernel(tail_rows, a_ref, b_ref, o_ref):
    a = a_ref[0]
    shifted = jnp.concatenate([a[tail_rows:], b_ref[0]], axis=0)
    rows = jnp.where(pl.program_id(1) == 0, a, shifted) if tail_rows else a
    o_ref[0] = rows.T.astype(BF16)


def _split_w_in(w_in):
    depth, d, _ = w_in.shape
    w_t = jnp.swapaxes(w_in, 1, 2)
    step = W_SPLIT_COLS
    tail = IN_SIZES[2]
    xbc_lo, dt_lo = IN_SIZES[0], IN_SIZES[0] + IN_SIZES[1]
    assert xbc_lo == step and dt_lo % step == 0 and step % tail == 0

    def call(n_steps, a_map, b_map, tail_rows, name):
        return pl.pallas_call(
            functools.partial(_transpose_cast_kernel, tail_rows),
            grid=(depth, n_steps),
            in_specs=[pl.BlockSpec((1, step, d), a_map), pl.BlockSpec((1, tail, d), b_map)],
            out_specs=pl.BlockSpec((1, d, step), lambda i, c: (i, 0, c)),
            out_shape=jax.ShapeDtypeStruct((depth, d, n_steps * step), BF16),
            compiler_params=pltpu.CompilerParams(
                dimension_semantics=("arbitrary", "arbitrary"),
                vmem_limit_bytes=VMEM_LIMIT_BYTES),
            name=name,
        )(w_t, w_t)

    first_rest = dt_lo // step
    w_xbc = call(CONV_CH // step, lambda i, c: (i, xbc_lo // step + c, 0),
                 lambda i, c: (i, 0, 0), 0, "split_w_xbc")
    w_rest = call(REST_WIDTH // step,
                  lambda i, c: (i, jnp.where(c == 0, 0, first_rest - 1 + c), 0),
                  lambda i, c: (i, (first_rest + c) * (step // tail), 0), tail, "split_w_rest")
    w_dt_t = w_t[:, dt_lo:dt_lo + tail, :].astype(BF16)
    return w_xbc, w_rest, w_dt_t


def _layer(x, mem, layer, norm_g, w_in_parts, conv_w, conv_b, dt_bias, a_log, d_skip, ssd_norm_g,
           mem_norm_g, mix_weights, final_g, tables, apply_final_norm):
    w_xbc, w_rest, w_dtT = w_in_parts
    rest, xsT, bc, dtT = _in_proj(x, norm_g, w_xbc, w_rest, w_dtT, layer, conv_w, conv_b)
    ys = _ssd(xsT, bc, dtT, dt_bias, a_log, d_skip)
    return _mix(x, ys, rest, ssd_norm_g, mem, mem_norm_g, *tables, *mix_weights, layer, final_g,
                apply_final_norm)


def kernel(x, mem, norm_g, w_in, conv_w, conv_b, dt_bias, a_log, d_skip, ssd_norm_g, mem_norm_g,
           w_kv, p_ssd, p_fno, p_mem, w_out, final_g):
    depth = norm_g.shape[0]
    tables = _dft_tables(x.shape[1])
    w_in_parts = _split_w_in(w_in)
    mix_weights = tuple(w.astype(BF16) for w in (w_kv, p_ssd, p_fno, p_mem, w_out))
    for i in range(depth):
        x = _layer(x, mem, i, norm_g[i], w_in_parts, conv_w[i], conv_b[i], dt_bias[i], a_log[i],
                   d_skip[i], ssd_norm_g[i], mem_norm_g[i], mix_weights, final_g, tables,
                   i == depth - 1)
    return x
```

```python
import functools
import math

import jax
import jax.numpy as jnp
from jax import lax
from jax.experimental import pallas as pl
from jax.experimental.pallas import tpu as pltpu

F32 = jnp.float32
BF16 = jnp.bfloat16
HIGHEST = lax.Precision.HIGHEST

D_MODEL = 1024
D_SSD = 1024
SSD_HEAD_DIM = 64
SSD_HEADS = D_SSD // SSD_HEAD_DIM
SSD_GROUPS = 4
HEADS_PER_GROUP = SSD_HEADS // SSD_GROUPS
GROUP_WIDTH = HEADS_PER_GROUP * SSD_HEAD_DIM
SSD_STATE = 128
BC_WIDTH = SSD_GROUPS * SSD_STATE
D_CONV = 5
CONV_CH = D_SSD + 2 * BC_WIDTH
CHUNK = 128
D_FNO = 512
FNO_GROUPS = 4
FNO_GROUP_DIM = D_FNO // FNO_GROUPS
N_MEM = 256
MEM_HEADS = 4
D_MEM = 512
MEM_HEAD_DIM = D_MEM // MEM_HEADS
N_BRANCH = 3
EPS = 1e-6
IN_SIZES = (D_SSD, CONV_CH, 2 * SSD_HEADS, D_FNO, D_FNO, D_MEM, D_MEM, N_BRANCH * D_MODEL)
REST_WIDTH = sum(IN_SIZES) - CONV_CH - 2 * SSD_HEADS
XBC_STEP = 256
REST_STEP = REST_WIDTH // (CONV_CH // XBC_STEP)

SUBLANES = 8
HALO_ROWS = 16
CONV_STRIDE = 4
CONV_STRIP = 128
CONV_HALO = D_CONV // 2
IN_PROJ_ROWS = 512
MIX_ROWS = 512
W_SPLIT_COLS = 1024
FWD_UNROLL = 8
BWD_UNROLL = 4
VMEM_LIMIT_BYTES = 56 * 1024 * 1024


def _sigmoid(v):
    return 0.5 * jnp.tanh(0.5 * v) + 0.5


def _silu(v):
    h = 0.5 * v
    return h * (jnp.tanh(h) + 1.0)


def _softplus(v):
    return jnp.maximum(v, 0.0) + jnp.log1p(jnp.exp(-jnp.abs(v)))


def _rms_scale(v):
    return v * lax.rsqrt(jnp.mean(v * v, axis=-1, keepdims=True) + EPS)


def _dot(a, b):
    return jnp.dot(a, b, preferred_element_type=F32)


def _dot_nt(a, b):
    return lax.dot_general(a, b, (((1,), (1,)), ((), ())), preferred_element_type=F32)


def _dot_exact(a, b):
    return jnp.dot(a, b, precision=HIGHEST, preferred_element_type=F32)


def _in_proj_kernel(tiles_per_seq,
                    x_ref, xprev_ref, xnext_ref, g_ref,
                    wxbc_ref, wrest_ref, wdtT_ref, convw_ref, convb_ref,
                    rest_ref, xsT_ref, bc_ref, dtT_ref,
                    stage_ref, act_ref, h_ref):
    rows = x_ref.shape[0]
    lt = pl.program_id(0) % tiles_per_seq
    g = g_ref[...]
    keep_prev = (lt > 0).astype(F32)
    keep_next = (lt < tiles_per_seq - 1).astype(F32)
    h_ref[0:HALO_ROWS, :] = (_rms_scale(xprev_ref[...]) * g * keep_prev).astype(BF16)
    h_ref[HALO_ROWS:HALO_ROWS + rows, :] = (_rms_scale(x_ref[...]) * g).astype(BF16)
    h_ref[HALO_ROWS + rows:, :] = (_rms_scale(xnext_ref[...]) * g * keep_next).astype(BF16)
    h_own = h_ref.at[HALO_ROWS:HALO_ROWS + rows, :]

    strips_per_step = XBC_STEP // CONV_STRIP

    def project_xbc(j):
        res = _dot(h_ref[...], wxbc_ref[:, pl.ds(j * XBC_STEP, XBC_STEP)])
        for t in range(strips_per_step):
            stage_ref[j * strips_per_step + t] = res[:, t * CONV_STRIP:(t + 1) * CONV_STRIP]

    def project_rest(j):
        cols = pl.ds(j * REST_STEP, REST_STEP)
        rest_ref[:, cols] = _dot(h_own[...], wrest_ref[:, cols]).astype(BF16)

    def conv(j, is_x):
        block = SUBLANES * CONV_STRIDE
        for t in range(strips_per_step):
            slab = j * strips_per_step + t
            cs = pl.ds(slab * CONV_STRIP, CONV_STRIP)
            bias = convb_ref[:, cs]
            taps = [convw_ref[k:k + 1, cs] for k in range(D_CONV)]
            for r0 in range(0, rows, block):
                base = HALO_ROWS + r0 - CONV_HALO
                regs = [stage_ref[slab, pl.ds(base + m, SUBLANES, stride=CONV_STRIDE), :]
                        for m in range(CONV_STRIDE + 2 * CONV_HALO)]
                for i in range(CONV_STRIDE):
                    acc = bias + taps[0] * regs[i]
                    for k in range(1, D_CONV):
                        acc = acc + taps[k] * regs[i + k]
                    act_ref[t, pl.ds(r0 + i, SUBLANES, stride=CONV_STRIDE), :] = _silu(acc)
            if is_x:
                for r0 in range(0, rows, CHUNK):
                    xsT_ref[0, cs, r0:r0 + CHUNK] = act_ref[t, r0:r0 + CHUNK, :].T.astype(BF16)
            else:
                bc_ref[:, pl.ds(slab * CONV_STRIP - D_SSD, CONV_STRIP)] = act_ref[t].astype(BF16)

    project_xbc(0)
    dtT_ref[...] = _dot_nt(wdtT_ref[...], h_own[...])

    n_steps = CONV_CH // XBC_STEP
    for j in range(n_steps):
        conv(j, j * XBC_STEP < D_SSD)
        if j + 1 < n_steps:
            project_xbc(j + 1)
        project_rest(j)


def _const_spec(shape):
    nd = len(shape)
    return pl.BlockSpec(shape, lambda *_: (0,) * nd, pipeline_mode=pl.Buffered(1))


def _layer_spec(stacked, layer):
    return pl.BlockSpec((None,) + stacked.shape[1:], lambda *_: (layer, 0, 0),
                        pipeline_mode=pl.Buffered(1))


def _in_proj(x, norm_g, w_xbc, w_rest, w_dtT, layer, conv_w, conv_b):
    bsz, seq, d = x.shape
    m = bsz * seq
    rows = IN_PROJ_ROWS
    tiles_per_seq = seq // rows
    n_tiles = m // rows
    halo_per_tile = rows // HALO_ROWS
    n_halo = m // HALO_ROWS
    x2 = x.reshape(m, d)

    def row_spec(width):
        return pl.BlockSpec((rows, width), lambda i: (i, 0))

    in_specs = [
        row_spec(d),
        pl.BlockSpec((HALO_ROWS, d), lambda i: (jnp.maximum(i * halo_per_tile - 1, 0), 0)),
        pl.BlockSpec((HALO_ROWS, d),
                     lambda i: (jnp.minimum((i + 1) * halo_per_tile, n_halo - 1), 0)),
        _const_spec((1, d)),
        _layer_spec(w_xbc, layer), _layer_spec(w_rest, layer), _layer_spec(w_dtT, layer),
        _const_spec(conv_w.shape), _const_spec((1, CONV_CH)),
    ]
    out_shape = [
        jax.ShapeDtypeStruct((m, REST_WIDTH), BF16),
        jax.ShapeDtypeStruct((bsz, D_SSD, seq), BF16),
        jax.ShapeDtypeStruct((m, 2 * BC_WIDTH), BF16),
        jax.ShapeDtypeStruct((2 * SSD_HEADS, m), F32),
    ]
    out_specs = [
        row_spec(REST_WIDTH),
        pl.BlockSpec((1, D_SSD, rows), lambda i: (i // tiles_per_seq, 0, i % tiles_per_seq)),
        row_spec(2 * BC_WIDTH),
        pl.BlockSpec((2 * SSD_HEADS, rows), lambda i: (0, i)),
    ]
    return pl.pallas_call(
        functools.partial(_in_proj_kernel, tiles_per_seq),
        grid=(n_tiles,),
        in_specs=in_specs,
        out_specs=out_specs,
        out_shape=out_shape,
        scratch_shapes=[pltpu.VMEM((CONV_CH // CONV_STRIP, rows + 2 * HALO_ROWS, CONV_STRIP), F32),
                        pltpu.VMEM((XBC_STEP // CONV_STRIP, rows, CONV_STRIP), F32),
                        pltpu.VMEM((rows + 2 * HALO_ROWS, d), BF16)],
        compiler_params=pltpu.CompilerParams(
            dimension_semantics=("arbitrary",), vmem_limit_bytes=VMEM_LIMIT_BYTES),
        name="in_proj",
    )(x2, x2, x2, norm_g.reshape(1, d), w_xbc, w_rest, w_dtT, conv_w, conv_b.reshape(1, CONV_CH))


def _expand_rows(v):
    lanes = v.shape[1]
    return jnp.concatenate(
        [jnp.broadcast_to(v[j:j + 1, :], (SSD_HEAD_DIM, lanes)) for j in range(v.shape[0])], axis=0)


TAB_QF, TAB_QB, TAB_EF, TAB_EB, TAB_WF, TAB_WB, TAB_DECAY_F, TAB_DECAY_B, TAB_DTB = range(9)
LOG2E = 1.4426950408889634


def _ssd_kernel(n_chunks,
                xsT_ref, b_ref, c_ref, dtT_ref, bias_col_ref, alog_col_ref, dskip_ref,
                y_ref,
                snap_ref, state_ref, tab_ref, colt_ref):
    H = SSD_HEADS
    row_id = lax.broadcasted_iota(jnp.int32, (CHUNK, CHUNK), 0)
    col_id = lax.broadcasted_iota(jnp.int32, (CHUNK, CHUNK), 1)
    lower = row_id >= col_id
    diag = row_id == col_id
    lower_f = lower.astype(F32)
    upper_f = (row_id <= col_id).astype(F32)

    dt_all = _softplus(dtT_ref[...] + bias_col_ref[...])
    a_all = dt_all * -jnp.exp(alog_col_ref[...])

    def stack(v, lo):
        return jnp.concatenate(
            [v[lo:lo + H, c * CHUNK:(c + 1) * CHUNK] for c in range(n_chunks)], axis=0)

    dt_f, dt_b = stack(dt_all, 0), stack(dt_all, H)
    cum_f = _dot_exact(stack(a_all, 0), upper_f)
    cum_b = _dot_exact(stack(a_all, H), lower_f)
    total_f = jnp.broadcast_to(cum_f[:, CHUNK - 1:CHUNK], cum_f.shape)
    total_b = jnp.broadcast_to(cum_b[:, 0:1], cum_b.shape)
    p_f = cum_f * LOG2E
    p_b = cum_b * LOG2E
    tab_ref[TAB_QF] = p_f - jnp.log2(dt_f)
    tab_ref[TAB_QB] = p_b - jnp.log2(dt_b)
    tab_ref[TAB_EF] = jnp.exp(cum_f)
    tab_ref[TAB_EB] = jnp.exp(cum_b)
    tab_ref[TAB_WF] = dt_f * jnp.exp(total_f - cum_f)
    tab_ref[TAB_WB] = dt_b * jnp.exp(total_b - cum_b)
    tab_ref[TAB_DECAY_F] = jnp.exp(total_f)
    tab_ref[TAB_DECAY_B] = jnp.exp(total_b)
    tab_ref[TAB_DTB] = dt_b
    pad = jnp.zeros((CHUNK - 2 * H, CHUNK), F32)
    for c in range(n_chunks):
        rows = slice(c * H, (c + 1) * H)
        colt_ref[c] = jnp.concatenate([p_f[rows], p_b[rows], pad], axis=0).T

    def chunk_rows(k, c):
        return tab_ref[k, pl.ds(pl.multiple_of(c * H, H), H), :]

    def state_update(c, g, xs_t, w, decay):
        hs = slice(g * HEADS_PER_GROUP, (g + 1) * HEADS_PER_GROUP)
        rs = slice(g * GROUP_WIDTH, (g + 1) * GROUP_WIDTH)
        xd_t = (xs_t * _expand_rows(w[hs])).astype(BF16)
        b_g = b_ref[0, pl.ds(c * CHUNK, CHUNK), g * SSD_STATE:(g + 1) * SSD_STATE]
        state_ref[rs, :] = _expand_rows(decay[hs]) * state_ref[rs, :] + _dot(xd_t, b_g)

    state_ref[...] = jnp.zeros_like(state_ref)

    def fwd_body(c, carry):
        snap_ref[c] = state_ref[...].astype(BF16)
        w, decay = chunk_rows(TAB_WF, c), chunk_rows(TAB_DECAY_F, c)
        for g in range(SSD_GROUPS):
            rs = slice(g * GROUP_WIDTH, (g + 1) * GROUP_WIDTH)
            xs_t = xsT_ref[0, rs, pl.ds(c * CHUNK, CHUNK)].astype(F32)
            state_update(c, g, xs_t, w, decay)
        return carry

    lax.fori_loop(0, n_chunks, fwd_body, 0, unroll=math.gcd(n_chunks, FWD_UNROLL))

    state_ref[...] = jnp.zeros_like(state_ref)

    def bwd_body(it, carry):
        c = n_chunks - 1 - it
        q_f, q_b = chunk_rows(TAB_QF, c), chunk_rows(TAB_QB, c)
        e_f, e_b = chunk_rows(TAB_EF, c), chunk_rows(TAB_EB, c)
        w_b, decay_b = chunk_rows(TAB_WB, c), chunk_rows(TAB_DECAY_B, c)
        dt_b_c = chunk_rows(TAB_DTB, c)
        groups = []
        for g in range(SSD_GROUPS):
            hs = slice(g * HEADS_PER_GROUP, (g + 1) * HEADS_PER_GROUP)
            rs = slice(g * GROUP_WIDTH, (g + 1) * GROUP_WIDTH)
            b_g = b_ref[0, pl.ds(c * CHUNK, CHUNK), g * SSD_STATE:(g + 1) * SSD_STATE]
            c_g = c_ref[0, pl.ds(c * CHUNK, CHUNK), g * SSD_STATE:(g + 1) * SSD_STATE]
            xs_bf = xsT_ref[0, rs, pl.ds(c * CHUNK, CHUNK)]
            xs_t = xs_bf.astype(F32)
            cb = _dot_nt(c_g, b_g)
            states = jnp.concatenate([snap_ref[c, rs, :], state_ref[rs, :].astype(BF16)], axis=0)
            y_off = _dot_nt(states, c_g)
            state_update(c, g, xs_t, w_b, decay_b)
            groups.append((hs, rs, xs_bf, xs_t, cb, y_off))
        for g, (hs, rs, xs_bf, xs_t, cb, y_off) in enumerate(groups):
            y_parts = []
            for j in range(HEADS_PER_GROUP):
                hd = g * HEADS_PER_GROUP + j
                arg = jnp.where(lower,
                                colt_ref[c, :, hd:hd + 1] - q_f[hd:hd + 1, :],
                                colt_ref[c, :, H + hd:H + hd + 1] - q_b[hd:hd + 1, :])
                m_h = (cb * jnp.exp2(arg)).astype(BF16)
                y_parts.append(_dot_nt(xs_bf[j * SSD_HEAD_DIM:(j + 1) * SSD_HEAD_DIM, :], m_h))
            cb_diag = jnp.sum(jnp.where(diag, cb, 0.0), axis=0, keepdims=True)
            own = dskip_ref[rs, :] + _expand_rows(cb_diag * dt_b_c[hs])
            y_t = (jnp.concatenate(y_parts, axis=0)
                   + y_off[:GROUP_WIDTH] * _expand_rows(e_f[hs])
                   + y_off[GROUP_WIDTH:] * _expand_rows(e_b[hs])
                   + own * xs_t)
            y_ref[0, pl.ds(c * CHUNK, CHUNK), rs] = y_t.T.astype(BF16)
        return carry

    lax.fori_loop(0, n_chunks, bwd_body, 0, unroll=math.gcd(n_chunks, BWD_UNROLL))


def _ssd(xsT, bc, dtT, dt_bias, a_log, d_skip):
    bsz, _, seq = xsT.shape
    n_chunks = seq // CHUNK
    two_h = 2 * SSD_HEADS
    bias_col = jnp.broadcast_to(dt_bias.reshape(two_h, 1).astype(F32), (two_h, seq))
    alog_col = jnp.broadcast_to(a_log.reshape(two_h, 1).astype(F32), (two_h, seq))
    dskip_col = jnp.broadcast_to(
        jnp.repeat(d_skip.astype(F32), SSD_HEAD_DIM)[:, None], (D_SSD, CHUNK))
    bc3 = bc.reshape(bsz, seq, 2 * BC_WIDTH)
    return pl.pallas_call(
        functools.partial(_ssd_kernel, n_chunks),
        grid=(bsz,),
        in_specs=[
            pl.BlockSpec((1, D_SSD, seq), lambda b: (b, 0, 0)),
            pl.BlockSpec((1, seq, BC_WIDTH), lambda b: (b, 0, 0)),
            pl.BlockSpec((1, seq, BC_WIDTH), lambda b: (b, 0, 1)),
            pl.BlockSpec((two_h, seq), lambda b: (0, b)),
            _const_spec((two_h, seq)), _const_spec((two_h, seq)),
            _const_spec((D_SSD, CHUNK)),
        ],
        out_specs=pl.BlockSpec((1, seq, D_SSD), lambda b: (b, 0, 0)),
        out_shape=jax.ShapeDtypeStruct((bsz, seq, D_SSD), BF16),
        scratch_shapes=[pltpu.VMEM((n_chunks, D_SSD, SSD_STATE), BF16),
                        pltpu.VMEM((D_SSD, SSD_STATE), F32),
                        pltpu.VMEM((TAB_DTB + 1, n_chunks * SSD_HEADS, CHUNK), F32),
                        pltpu.VMEM((n_chunks, CHUNK, CHUNK), F32)],
        compiler_params=pltpu.CompilerParams(
            dimension_semantics=("arbitrary",), vmem_limit_bytes=VMEM_LIMIT_BYTES),
        name="ssd",
    )(xsT, bc3, bc3, dtT, bias_col, alog_col, dskip_col)


def _mix_kernel(apply_final_norm,
                x_ref, ys_ref, zs_ref, ssdg_ref, uf_ref, zf_ref, q_ref, zm_ref, gates_ref,
                mem_ref, memg_ref, wkv_ref, chan_dft_ref, seq_dft_ref,
                pssd_ref, pfno_ref, pmem_ref, wout_ref, finalg_ref,
                out_ref,
                ab_ref, kv_ref, dft_ref, mid_ref):
    seq = uf_ref.shape[1]
    lt = pl.program_id(1)

    @pl.when(lt == 0)
    def _per_sequence():
        half = seq // 2
        r_id = lax.broadcasted_iota(jnp.int32, (CHUNK, 2 * CHUNK), 0)
        c_id = lax.broadcasted_iota(jnp.int32, (CHUNK, 2 * CHUNK), 1)
        reverse = (r_id + c_id == CHUNK).astype(BF16)
        sign = jnp.concatenate([jnp.ones((1, D_FNO), F32), -jnp.ones((1, D_FNO), F32)], axis=1)
        for gi in range(FNO_GROUPS):
            cs = slice(gi * FNO_GROUP_DIM, (gi + 1) * FNO_GROUP_DIM)
            t = _dot(uf_ref[0, :, cs], chan_dft_ref[...]).astype(BF16)
            dft_ref[:, cs] = t[:, :FNO_GROUP_DIM]
            dft_ref[:, D_FNO + gi * FNO_GROUP_DIM:D_FNO + (gi + 1) * FNO_GROUP_DIM] = (
                t[:, FNO_GROUP_DIM:])
        mid_ref[0:1, :] = dft_ref[half:half + 1, :D_FNO].astype(F32)
        for blk in range(half // CHUNK):
            lo = seq - (blk + 1) * CHUNK
            if blk == 0:
                window = jnp.concatenate(
                    [dft_ref[lo:, :], jnp.zeros((CHUNK, 2 * D_FNO), BF16)], axis=0)
            else:
                window = dft_ref[lo:lo + 2 * CHUNK, :]
            own = dft_ref[blk * CHUNK:(blk + 1) * CHUNK, :].astype(F32)
            folded = (own + sign * _dot(reverse, window)).astype(BF16)
            ab_ref[blk * CHUNK:(blk + 1) * CHUNK, :] = folded[:, :D_FNO]
            ab_ref[half + blk * CHUNK:half + (blk + 1) * CHUNK, :] = folded[:, D_FNO:]
        hm = (_rms_scale(mem_ref[0]) * memg_ref[...]).astype(BF16)
        kv_ref[...] = _dot(hm, wkv_ref[...]).astype(BF16)

    yf = _dot(seq_dft_ref[...], ab_ref[...])
    q = q_ref[0]
    scores = []
    for hd in range(MEM_HEADS):
        cs = slice(hd * MEM_HEAD_DIM, (hd + 1) * MEM_HEAD_DIM)
        scores.append(_dot_nt(q[:, cs], kv_ref[:, cs]))

    t = ys_ref[0].astype(F32) * _silu(zs_ref[0].astype(F32))
    parts = []
    for gi in range(SSD_GROUPS):
        parts.append(_rms_scale(t[:, gi * GROUP_WIDTH:(gi + 1) * GROUP_WIDTH]))
    ysn = (jnp.concatenate(parts, axis=1) * ssdg_ref[...]).astype(BF16)
    m_s = _dot(ysn, pssd_ref[...])

    k_id = lax.broadcasted_iota(jnp.int32, (x_ref.shape[1], D_FNO), 0)
    alt = jnp.where(k_id % 2 == 0, seq ** -0.5, -(seq ** -0.5))
    yf = (yf + alt * mid_ref[0:1, :]) * _silu(zf_ref[0].astype(F32))
    m_f = _dot(yf.astype(BF16), pfno_ref[...])

    heads = []
    for hd in range(MEM_HEADS):
        s = scores[hd] * (MEM_HEAD_DIM ** -0.5)
        p = jnp.exp(s - jnp.max(s, axis=-1, keepdims=True))
        denom = jnp.sum(p, axis=-1, keepdims=True)
        pv = _dot(p.astype(BF16), kv_ref[:, D_MEM + hd * MEM_HEAD_DIM:D_MEM + (hd + 1) * MEM_HEAD_DIM])
        heads.append(pv / denom)
    ym = jnp.concatenate(heads, axis=1) * _silu(zm_ref[0].astype(F32))
    m_m = _dot(ym.astype(BF16), pmem_ref[...])

    gate = _sigmoid(gates_ref[0].astype(F32))
    merged = (gate[:, :D_MODEL] * m_s + gate[:, D_MODEL:2 * D_MODEL] * m_f
              + gate[:, 2 * D_MODEL:] * m_m)
    out = x_ref[0] + _dot(merged.astype(BF16), wout_ref[...])
    if apply_final_norm:
        out = _rms_scale(out) * finalg_ref[...]
    out_ref[0] = out


def _mix(x, ys, rest, ssd_norm_g, mem, mem_norm_g, chan_dft, seq_dft, w_kv, p_ssd, p_fno, p_mem,
         w_out, layer, final_g, apply_final_norm):
    bsz, seq, d = x.shape
    rows = MIX_ROWS
    n_tiles = seq // rows
    rest3 = rest.reshape(bsz, seq, REST_WIDTH)

    def tile_spec(width, col_block=0):
        return pl.BlockSpec((1, rows, width), lambda b, t: (b, t, col_block))

    assert D_SSD == 2 * D_FNO and D_FNO == D_MEM and REST_WIDTH == 2 * N_BRANCH * D_MODEL
    in_specs = [
        tile_spec(d), tile_spec(D_SSD), tile_spec(D_SSD, 0), _const_spec((1, D_SSD)),
        pl.BlockSpec((1, seq, D_FNO), lambda b, t: (b, 0, 2)),
        tile_spec(D_FNO, 3), tile_spec(D_MEM, 4), tile_spec(D_MEM, 5),
        tile_spec(N_BRANCH * D_MODEL, 1),
        pl.BlockSpec((1, N_MEM, d), lambda b, t: (b, 0, 0)),
        _const_spec((1, d)), _layer_spec(w_kv, layer), _const_spec(chan_dft.shape),
        pl.BlockSpec((rows, seq), lambda b, t: (t, 0)),
        _layer_spec(p_ssd, layer), _layer_spec(p_fno, layer), _layer_spec(p_mem, layer),
        _layer_spec(w_out, layer), _const_spec((1, d)),
    ]
    return pl.pallas_call(
        functools.partial(_mix_kernel, apply_final_norm),
        grid=(bsz, n_tiles),
        in_specs=in_specs,
        out_specs=tile_spec(d),
        out_shape=jax.ShapeDtypeStruct((bsz, seq, d), F32),
        scratch_shapes=[pltpu.VMEM((seq, D_FNO), BF16),
                        pltpu.VMEM((N_MEM, 2 * D_MEM), BF16),
                        pltpu.VMEM((seq, 2 * D_FNO), BF16),
                        pltpu.VMEM((SUBLANES, D_FNO), F32)],
        compiler_params=pltpu.CompilerParams(
            dimension_semantics=("arbitrary", "arbitrary"), vmem_limit_bytes=VMEM_LIMIT_BYTES),
        name="mix",
    )(x, ys, rest3, ssd_norm_g.reshape(1, D_SSD), rest3, rest3, rest3, rest3, rest3, mem,
      mem_norm_g.reshape(1, d), w_kv, chan_dft, seq_dft, p_ssd, p_fno, p_mem, w_out,
      final_g.reshape(1, d))


def _dft_table(n, split, sin_sign, n_cols):
    m = jnp.arange(n_cols, dtype=jnp.int32)
    k_hi = jnp.arange(n // split, dtype=jnp.int32) * split
    k_lo = jnp.arange(split, dtype=jnp.int32)
    ang_hi = (2.0 * math.pi / n) * ((k_hi[:, None] * m[None, :]) % n).astype(F32)
    ang_lo = (2.0 * math.pi / n) * ((k_lo[:, None] * m[None, :]) % n).astype(F32)
    ca, sa = jnp.cos(ang_hi), jnp.sin(ang_hi)
    cb, sb = jnp.cos(ang_lo), jnp.sin(ang_lo)
    u = jnp.concatenate([ca, sin_sign * sa], axis=1)[:, None, :]
    v = jnp.concatenate([-sa, sin_sign * ca], axis=1)[:, None, :]
    cb2 = jnp.concatenate([cb, cb], axis=1)[None, :, :]
    sb2 = jnp.concatenate([sb, sb], axis=1)[None, :, :]
    return ((u * cb2 + v * sb2) * (n ** -0.5)).astype(BF16).reshape(n, 2 * n_cols)


def _dft_tables(seq):
    return (_dft_table(FNO_GROUP_DIM, 8, 1.0, FNO_GROUP_DIM),
            _dft_table(seq, 64, -1.0, seq // 2))


def _transpose_cast_kernel(tail_rows, a_ref, b_ref, o_ref):
    a = a_ref[0]
    shifted = jnp.concatenate([a[tail_rows:], b_ref[0]], axis=0)
    rows = jnp.where(pl.program_id(1) == 0, a, shifted) if tail_rows else a
    o_ref[0] = rows.T.astype(BF16)


def _split_w_in(w_in):
    depth, d, _ = w_in.shape
    w_t = jnp.swapaxes(w_in, 1, 2)
    step = W_SPLIT_COLS
    tail = IN_SIZES[2]
    xbc_lo, dt_lo = IN_SIZES[0], IN_SIZES[0] + IN_SIZES[1]
    assert xbc_lo == step and dt_lo % step == 0 and step % tail == 0

    def call(n_steps, a_map, b_map, tail_rows, name):
        return pl.pallas_call(
            functools.partial(_transpose_cast_kernel, tail_rows),
            grid=(depth, n_steps),
            in_specs=[pl.BlockSpec((1, step, d), a_map), pl.BlockSpec((1, tail, d), b_map)],
            out_specs=pl.BlockSpec((1, d, step), lambda i, c: (i, 0, c)),
            out_shape=jax.ShapeDtypeStruct((depth, d, n_steps * step), BF16),
            compiler_params=pltpu.CompilerParams(
                dimension_semantics=("arbitrary", "arbitrary"),
                vmem_limit_bytes=VMEM_LIMIT_BYTES),
            name=name,
        )(w_t, w_t)

    first_rest = dt_lo // step
    w_xbc = call(CONV_CH // step, lambda i, c: (i, xbc_lo // step + c, 0),
                 lambda i, c: (i, 0, 0), 0, "split_w_xbc")
    w_rest = call(REST_WIDTH // step,
                  lambda i, c: (i, jnp.where(c == 0, 0, first_rest - 1 + c), 0),
                  lambda i, c: (i, (first_rest + c) * (step // tail), 0), tail, "split_w_rest")
    w_dt_t = w_t[:, dt_lo:dt_lo + tail, :].astype(BF16)
    return w_xbc, w_rest, w_dt_t


def _layer(x, mem, layer, norm_g, w_in_parts, conv_w, conv_b, dt_bias, a_log, d_skip, ssd_norm_g,
           mem_norm_g, mix_weights, final_g, tables, apply_final_norm):
    w_xbc, w_rest, w_dtT = w_in_parts
    rest, xsT, bc, dtT = _in_proj(x, norm_g, w_xbc, w_rest, w_dtT, layer, conv_w, conv_b)
    ys = _ssd(xsT, bc, dtT, dt_bias, a_log, d_skip)
    return _mix(x, ys, rest, ssd_norm_g, mem, mem_norm_g, *tables, *mix_weights, layer, final_g,
                apply_final_norm)


def kernel(x, mem, norm_g, w_in, conv_w, conv_b, dt_bias, a_log, d_skip, ssd_norm_g, mem_norm_g,
           w_kv, p_ssd, p_fno, p_mem, w_out, final_g):
    depth = norm_g.shape[0]
    tables = _dft_tables(x.shape[1])
    w_in_parts = _split_w_in(w_in)
    mix_weights = tuple(w.astype(BF16) for w in (w_kv, p_ssd, p_fno, p_mem, w_out))
    for i in range(depth):
        x = _layer(x, mem, i, norm_g[i], w_in_parts, conv_w[i], conv_b[i], dt_bias[i], a_log[i],
                   d_skip[i], ssd_norm_g[i], mem_norm_g[i], mix_weights, final_g, tables,
                   i == depth - 1)
    return x
```

```python
import functools
import math

import jax
import jax.numpy as jnp
from jax import lax
from jax.experimental import pallas as pl
from jax.experimental.pallas import tpu as pltpu

F32 = jnp.float32
BF16 = jnp.bfloat16
HIGHEST = lax.Precision.HIGHEST

D_MODEL = 1024
D_SSD = 1024
SSD_HEAD_DIM = 64
SSD_HEADS = D_SSD // SSD_HEAD_DIM
SSD_GROUPS = 4
HEADS_PER_GROUP = SSD_HEADS // SSD_GROUPS
GROUP_WIDTH = HEADS_PER_GROUP * SSD_HEAD_DIM
SSD_STATE = 128
BC_WIDTH = SSD_GROUPS * SSD_STATE
D_CONV = 5
CONV_CH = D_SSD + 2 * BC_WIDTH
CHUNK = 128
D_FNO = 512
FNO_GROUPS = 4
FNO_GROUP_DIM = D_FNO // FNO_GROUPS
N_MEM = 256
MEM_HEADS = 4
D_MEM = 512
MEM_HEAD_DIM = D_MEM // MEM_HEADS
N_BRANCH = 3
EPS = 1e-6
IN_SIZES = (D_SSD, CONV_CH, 2 * SSD_HEADS, D_FNO, D_FNO, D_MEM, D_MEM, N_BRANCH * D_MODEL)
REST_WIDTH = sum(IN_SIZES) - CONV_CH - 2 * SSD_HEADS
XBC_STEP = 256
REST_STEP = REST_WIDTH // (CONV_CH // XBC_STEP)

SUBLANES = 8
HALO_ROWS = 16
CONV_STRIDE = 4
CONV_STRIP = 128
CONV_HALO = D_CONV // 2
IN_PROJ_ROWS = 512
MIX_ROWS = 512
W_SPLIT_COLS = 1024
FWD_UNROLL = 8
BWD_UNROLL = 8
VMEM_LIMIT_BYTES = 56 * 1024 * 1024


def _sigmoid(v):
    return 0.5 * jnp.tanh(0.5 * v) + 0.5


def _silu(v):
    h = 0.5 * v
    return h * (jnp.tanh(h) + 1.0)


def _softplus(v):
    return jnp.maximum(v, 0.0) + jnp.log1p(jnp.exp(-jnp.abs(v)))


def _rms_scale(v):
    return v * lax.rsqrt(jnp.mean(v * v, axis=-1, keepdims=True) + EPS)


def _dot(a, b):
    return jnp.dot(a, b, preferred_element_type=F32)


def _dot_nt(a, b):
    return lax.dot_general(a, b, (((1,), (1,)), ((), ())), preferred_element_type=F32)


def _dot_exact(a, b):
    return jnp.dot(a, b, precision=HIGHEST, preferred_element_type=F32)


def _in_proj_kernel(tiles_per_seq,
                    x_ref, xprev_ref, xnext_ref, g_ref,
                    wxbc_ref, wrest_ref, wdtT_ref, convw_ref, convb_ref,
                    rest_ref, xsT_ref, bc_ref, dtT_ref,
                    stage_ref, act_ref, h_ref):
    rows = x_ref.shape[0]
    lt = pl.program_id(0) % tiles_per_seq
    g = g_ref[...]
    keep_prev = (lt > 0).astype(F32)
    keep_next = (lt < tiles_per_seq - 1).astype(F32)
    h_ref[0:HALO_ROWS, :] = (_rms_scale(xprev_ref[...]) * g * keep_prev).astype(BF16)
    h_ref[HALO_ROWS:HALO_ROWS + rows, :] = (_rms_scale(x_ref[...]) * g).astype(BF16)
    h_ref[HALO_ROWS + rows:, :] = (_rms_scale(xnext_ref[...]) * g * keep_next).astype(BF16)
    h_own = h_ref.at[HALO_ROWS:HALO_ROWS + rows, :]

    strips_per_step = XBC_STEP // CONV_STRIP

    def project_xbc(j):
        res = _dot(h_ref[...], wxbc_ref[:, pl.ds(j * XBC_STEP, XBC_STEP)])
        for t in range(strips_per_step):
            stage_ref[j * strips_per_step + t] = res[:, t * CONV_STRIP:(t + 1) * CONV_STRIP]

    def project_rest(j):
        cols = pl.ds(j * REST_STEP, REST_STEP)
        rest_ref[:, cols] = _dot(h_own[...], wrest_ref[:, cols]).astype(BF16)

    def conv(j, is_x):
        block = SUBLANES * CONV_STRIDE
        for t in range(strips_per_step):
            slab = j * strips_per_step + t
            cs = pl.ds(slab * CONV_STRIP, CONV_STRIP)
            bias = convb_ref[:, cs]
            taps = [convw_ref[k:k + 1, cs] for k in range(D_CONV)]
            for r0 in range(0, rows, block):
                base = HALO_ROWS + r0 - CONV_HALO
                regs = [stage_ref[slab, pl.ds(base + m, SUBLANES, stride=CONV_STRIDE), :]
                        for m in range(CONV_STRIDE + 2 * CONV_HALO)]
                for i in range(CONV_STRIDE):
                    acc = bias + taps[0] * regs[i]
                    for k in range(1, D_CONV):
                        acc = acc + taps[k] * regs[i + k]
                    act_ref[t, pl.ds(r0 + i, SUBLANES, stride=CONV_STRIDE), :] = _silu(acc)
            if is_x:
                for r0 in range(0, rows, CHUNK):
                    xsT_ref[0, cs, r0:r0 + CHUNK] = act_ref[t, r0:r0 + CHUNK, :].T.astype(BF16)
            else:
                bc_ref[:, pl.ds(slab * CONV_STRIP - D_SSD, CONV_STRIP)] = act_ref[t].astype(BF16)

    project_xbc(0)
    dtT_ref[...] = _dot_nt(wdtT_ref[...], h_own[...])

    n_steps = CONV_CH // XBC_STEP
    for j in range(n_steps):
        conv(j, j * XBC_STEP < D_SSD)
        if j + 1 < n_steps:
            project_xbc(j + 1)
        project_rest(j)


def _const_spec(shape):
    nd = len(shape)
    return pl.BlockSpec(shape, lambda *_: (0,) * nd, pipeline_mode=pl.Buffered(1))


def _layer_spec(stacked, layer):
    return pl.BlockSpec((None,) + stacked.shape[1:], lambda *_: (layer, 0, 0),
                        pipeline_mode=pl.Buffered(1))


def _in_proj(x, norm_g, w_xbc, w_rest, w_dtT, layer, conv_w, conv_b):
    bsz, seq, d = x.shape
    m = bsz * seq
    rows = IN_PROJ_ROWS
    tiles_per_seq = seq // rows
    n_tiles = m // rows
    halo_per_tile = rows // HALO_ROWS
    n_halo = m // HALO_ROWS
    x2 = x.reshape(m, d)

    def row_spec(width):
        return pl.BlockSpec((rows, width), lambda i: (i, 0))

    in_specs = [
        row_spec(d),
        pl.BlockSpec((HALO_ROWS, d), lambda i: (jnp.maximum(i * halo_per_tile - 1, 0), 0)),
        pl.BlockSpec((HALO_ROWS, d),
                     lambda i: (jnp.minimum((i + 1) * halo_per_tile, n_halo - 1), 0)),
        _const_spec((1, d)),
        _layer_spec(w_xbc, layer), _layer_spec(w_rest, layer), _layer_spec(w_dtT, layer),
        _const_spec(conv_w.shape), _const_spec((1, CONV_CH)),
    ]
    out_shape = [
        jax.ShapeDtypeStruct((m, REST_WIDTH), BF16),
        jax.ShapeDtypeStruct((bsz, D_SSD, seq), BF16),
        jax.ShapeDtypeStruct((m, 2 * BC_WIDTH), BF16),
        jax.ShapeDtypeStruct((2 * SSD_HEADS, m), F32),
    ]
    out_specs = [
        row_spec(REST_WIDTH),
        pl.BlockSpec((1, D_SSD, rows), lambda i: (i // tiles_per_seq, 0, i % tiles_per_seq)),
        row_spec(2 * BC_WIDTH),
        pl.BlockSpec((2 * SSD_HEADS, rows), lambda i: (0, i)),
    ]
    return pl.pallas_call(
        functools.partial(_in_proj_kernel, tiles_per_seq),
        grid=(n_tiles,),
        in_specs=in_specs,
        out_specs=out_specs,
        out_shape=out_shape,
        scratch_shapes=[pltpu.VMEM((CONV_CH // CONV_STRIP, rows + 2 * HALO_ROWS, CONV_STRIP), F32),
                        pltpu.VMEM((XBC_STEP // CONV_STRIP, rows, CONV_STRIP), F32),
                        pltpu.VMEM((rows + 2 * HALO_ROWS, d), BF16)],
        compiler_params=pltpu.CompilerParams(
            dimension_semantics=("arbitrary",), vmem_limit_bytes=VMEM_LIMIT_BYTES),
        name="in_proj",
    )(x2, x2, x2, norm_g.reshape(1, d), w_xbc, w_rest, w_dtT, conv_w, conv_b.reshape(1, CONV_CH))


def _expand_rows(v):
    lanes = v.shape[1]
    return jnp.concatenate(
        [jnp.broadcast_to(v[j:j + 1, :], (SSD_HEAD_DIM, lanes)) for j in range(v.shape[0])], axis=0)


TAB_QF, TAB_QB, TAB_EF, TAB_EB, TAB_WF, TAB_WB, TAB_DECAY_F, TAB_DECAY_B, TAB_DTB = range(9)
LOG2E = 1.4426950408889634


def _ssd_kernel(n_chunks,
                xsT_ref, b_ref, c_ref, dtT_ref, bias_col_ref, alog_col_ref, dskip_ref,
                y_ref,
                snap_ref, state_ref, tab_ref, colt_ref):
    H = SSD_HEADS
    row_id = lax.broadcasted_iota(jnp.int32, (CHUNK, CHUNK), 0)
    col_id = lax.broadcasted_iota(jnp.int32, (CHUNK, CHUNK), 1)
    lower = row_id >= col_id
    diag = row_id == col_id
    lower_f = lower.astype(F32)
    upper_f = (row_id <= col_id).astype(F32)

    dt_all = _softplus(dtT_ref[...] + bias_col_ref[...])
    a_all = dt_all * -jnp.exp(alog_col_ref[...])

    def stack(v, lo):
        return jnp.concatenate(
            [v[lo:lo + H, c * CHUNK:(c + 1) * CHUNK] for c in range(n_chunks)], axis=0)

    dt_f, dt_b = stack(dt_all, 0), stack(dt_all, H)
    cum_f = _dot_exact(stack(a_all, 0), upper_f)
    cum_b = _dot_exact(stack(a_all, H), lower_f)
    total_f = jnp.broadcast_to(cum_f[:, CHUNK - 1:CHUNK], cum_f.shape)
    total_b = jnp.broadcast_to(cum_b[:, 0:1], cum_b.shape)
    p_f = cum_f * LOG2E
    p_b = cum_b * LOG2E
    tab_ref[TAB_QF] = p_f - jnp.log2(dt_f)
    tab_ref[TAB_QB] = p_b - jnp.log2(dt_b)
    tab_ref[TAB_EF] = jnp.exp(cum_f)
    tab_ref[TAB_EB] = jnp.exp(cum_b)
    tab_ref[TAB_WF] = dt_f * jnp.exp(total_f - cum_f)
    tab_ref[TAB_WB] = dt_b * jnp.exp(total_b - cum_b)
    tab_ref[TAB_DECAY_F] = jnp.exp(total_f)
    tab_ref[TAB_DECAY_B] = jnp.exp(total_b)
    tab_ref[TAB_DTB] = dt_b
    pad = jnp.zeros((CHUNK - 2 * H, CHUNK), F32)
    for c in range(n_chunks):
        rows = slice(c * H, (c + 1) * H)
        colt_ref[c] = jnp.concatenate([p_f[rows], p_b[rows], pad], axis=0).T

    def chunk_rows(k, c):
        return tab_ref[k, pl.ds(pl.multiple_of(c * H, H), H), :]

    def state_update(c, g, xs_t, w, decay):
        hs = slice(g * HEADS_PER_GROUP, (g + 1) * HEADS_PER_GROUP)
        rs = slice(g * GROUP_WIDTH, (g + 1) * GROUP_WIDTH)
        xd_t = (xs_t * _expand_rows(w[hs])).astype(BF16)
        b_g = b_ref[0, pl.ds(c * CHUNK, CHUNK), g * SSD_STATE:(g + 1) * SSD_STATE]
        state_ref[rs, :] = _expand_rows(decay[hs]) * state_ref[rs, :] + _dot(xd_t, b_g)

    state_ref[...] = jnp.zeros_like(state_ref)

    def fwd_body(c, carry):
        snap_ref[c] = state_ref[...].astype(BF16)
        w, decay = chunk_rows(TAB_WF, c), chunk_rows(TAB_DECAY_F, c)
        for g in range(SSD_GROUPS):
            rs = slice(g * GROUP_WIDTH, (g + 1) * GROUP_WIDTH)
            xs_t = xsT_ref[0, rs, pl.ds(c * CHUNK, CHUNK)].astype(F32)
            state_update(c, g, xs_t, w, decay)
        return carry

    lax.fori_loop(0, n_chunks, fwd_body, 0, unroll=math.gcd(n_chunks, FWD_UNROLL))

    state_ref[...] = jnp.zeros_like(state_ref)

    def bwd_body(it, carry):
        c = n_chunks - 1 - it
        q_f, q_b = chunk_rows(TAB_QF, c), chunk_rows(TAB_QB, c)
        e_f, e_b = chunk_rows(TAB_EF, c), chunk_rows(TAB_EB, c)
        w_b, decay_b = chunk_rows(TAB_WB, c), chunk_rows(TAB_DECAY_B, c)
        dt_b_c = chunk_rows(TAB_DTB, c)
        groups = []
        for g in range(SSD_GROUPS):
            hs = slice(g * HEADS_PER_GROUP, (g + 1) * HEADS_PER_GROUP)
            rs = slice(g * GROUP_WIDTH, (g + 1) * GROUP_WIDTH)
            b_g = b_ref[0, pl.ds(c * CHUNK, CHUNK), g * SSD_STATE:(g + 1) * SSD_STATE]
            c_g = c_ref[0, pl.ds(c * CHUNK, CHUNK), g * SSD_STATE:(g + 1) * SSD_STATE]
            xs_bf = xsT_ref[0, rs, pl.ds(c * CHUNK, CHUNK)]
            xs_t = xs_bf.astype(F32)
            cb = _dot_nt(c_g, b_g)
            states = jnp.concatenate([snap_ref[c, rs, :], state_ref[rs, :].astype(BF16)], axis=0)
            y_off = _dot_nt(states, c_g)
            state_update(c, g, xs_t, w_b, decay_b)
            groups.append((hs, rs, xs_bf, xs_t, cb, y_off))
        for g, (hs, rs, xs_bf, xs_t, cb, y_off) in enumerate(groups):
            y_parts = []
            for j in range(HEADS_PER_GROUP):
                hd = g * HEADS_PER_GROUP + j
                arg = jnp.where(lower,
                                colt_ref[c, :, hd:hd + 1] - q_f[hd:hd + 1, :],
                                colt_ref[c, :, H + hd:H + hd + 1] - q_b[hd:hd + 1, :])
                m_h = (cb * jnp.exp2(arg)).astype(BF16)
                y_parts.append(_dot_nt(xs_bf[j * SSD_HEAD_DIM:(j + 1) * SSD_HEAD_DIM, :], m_h))
            cb_diag = jnp.sum(jnp.where(diag, cb, 0.0), axis=0, keepdims=True)
            own = dskip_ref[rs, :] + _expand_rows(cb_diag * dt_b_c[hs])
            y_t = (jnp.concatenate(y_parts, axis=0)
                   + y_off[:GROUP_WIDTH] * _expand_rows(e_f[hs])
                   + y_off[GROUP_WIDTH:] * _expand_rows(e_b[hs])
                   + own * xs_t)
            y_ref[0, pl.ds(c * CHUNK, CHUNK), rs] = y_t.T.astype(BF16)
        return carry

    lax.fori_loop(0, n_chunks, bwd_body, 0, unroll=math.gcd(n_chunks, BWD_UNROLL))


def _ssd(xsT, bc, dtT, dt_bias, a_log, d_skip):
    bsz, _, seq = xsT.shape
    n_chunks = seq // CHUNK
    two_h = 2 * SSD_HEADS
    bias_col = jnp.broadcast_to(dt_bias.reshape(two_h, 1).astype(F32), (two_h, seq))
    alog_col = jnp.broadcast_to(a_log.reshape(two_h, 1).astype(F32), (two_h, seq))
    dskip_col = jnp.broadcast_to(
        jnp.repeat(d_skip.astype(F32), SSD_HEAD_DIM)[:, None], (D_SSD, CHUNK))
    bc3 = bc.reshape(bsz, seq, 2 * BC_WIDTH)
    return pl.pallas_call(
        functools.partial(_ssd_kernel, n_chunks),
        grid=(bsz,),
        in_specs=[
            pl.BlockSpec((1, D_SSD, seq), lambda b: (b, 0, 0)),
            pl.BlockSpec((1, seq, BC_WIDTH), lambda b: (b, 0, 0)),
            pl.BlockSpec((1, seq, BC_WIDTH), lambda b: (b, 0, 1)),
            pl.BlockSpec((two_h, seq), lambda b: (0, b)),
            _const_spec((two_h, seq)), _const_spec((two_h, seq)),
            _const_spec((D_SSD, CHUNK)),
        ],
        out_specs=pl.BlockSpec((1, seq, D_SSD), lambda b: (b, 0, 0)),
        out_shape=jax.ShapeDtypeStruct((bsz, seq, D_SSD), BF16),
        scratch_shapes=[pltpu.VMEM((n_chunks, D_SSD, SSD_STATE), BF16),
                        pltpu.VMEM((D_SSD, SSD_STATE), F32),
                        pltpu.VMEM((TAB_DTB + 1, n_chunks * SSD_HEADS, CHUNK), F32),
                        pltpu.VMEM((n_chunks, CHUNK, CHUNK), F32)],
        compiler_params=pltpu.CompilerParams(
            dimension_semantics=("arbitrary",), vmem_limit_bytes=VMEM_LIMIT_BYTES),
        name="ssd",
    )(xsT, bc3, bc3, dtT, bias_col, alog_col, dskip_col)


def _mix_kernel(apply_final_norm,
                x_ref, ys_ref, zs_ref, ssdg_ref, uf_ref, zf_ref, q_ref, zm_ref, gates_ref,
                mem_ref, memg_ref, wkv_ref, chan_dft_ref, seq_dft_ref,
                pssd_ref, pfno_ref, pmem_ref, wout_ref, finalg_ref,
                out_ref,
                ab_ref, kv_ref, dft_ref, mid_ref):
    seq = uf_ref.shape[1]
    lt = pl.program_id(1)

    @pl.when(lt == 0)
    def _per_sequence():
        half = seq // 2
        r_id = lax.broadcasted_iota(jnp.int32, (CHUNK, 2 * CHUNK), 0)
        c_id = lax.broadcasted_iota(jnp.int32, (CHUNK, 2 * CHUNK), 1)
        reverse = (r_id + c_id == CHUNK).astype(BF16)
        sign = jnp.concatenate([jnp.ones((1, D_FNO), F32), -jnp.ones((1, D_FNO), F32)], axis=1)
        for gi in range(FNO_GROUPS):
            cs = slice(gi * FNO_GROUP_DIM, (gi + 1) * FNO_GROUP_DIM)
            t = _dot(uf_ref[0, :, cs], chan_dft_ref[...]).astype(BF16)
            dft_ref[:, cs] = t[:, :FNO_GROUP_DIM]
            dft_ref[:, D_FNO + gi * FNO_GROUP_DIM:D_FNO + (gi + 1) * FNO_GROUP_DIM] = (
                t[:, FNO_GROUP_DIM:])
        mid_ref[0:1, :] = dft_ref[half:half + 1, :D_FNO].astype(F32)
        for blk in range(half // CHUNK):
            lo = seq - (blk + 1) * CHUNK
            if blk == 0:
                window = jnp.concatenate(
                    [dft_ref[lo:, :], jnp.zeros((CHUNK, 2 * D_FNO), BF16)], axis=0)
            else:
                window = dft_ref[lo:lo + 2 * CHUNK, :]
            own = dft_ref[blk * CHUNK:(blk + 1) * CHUNK, :].astype(F32)
            folded = (own + sign * _dot(reverse, window)).astype(BF16)
            ab_ref[blk * CHUNK:(blk + 1) * CHUNK, :] = folded[:, :D_FNO]
            ab_ref[half + blk * CHUNK:half + (blk + 1) * CHUNK, :] = folded[:, D_FNO:]
        hm = (_rms_scale(mem_ref[0]) * memg_ref[...]).astype(BF16)
        kv_ref[...] = _dot(hm, wkv_ref[...]).astype(BF16)

    tile_rows = pl.ds(pl.multiple_of(lt * x_ref.shape[1], x_ref.shape[1]), x_ref.shape[1])
    yf = _dot(seq_dft_ref[tile_rows, :], ab_ref[...])
    q = q_ref[0]
    scores = []
    for hd in range(MEM_HEADS):
        cs = slice(hd * MEM_HEAD_DIM, (hd + 1) * MEM_HEAD_DIM)
        scores.append(_dot_nt(q[:, cs], kv_ref[:, cs]))

    t = ys_ref[0].astype(F32) * _silu(zs_ref[0].astype(F32))
    parts = []
    for gi in range(SSD_GROUPS):
        parts.append(_rms_scale(t[:, gi * GROUP_WIDTH:(gi + 1) * GROUP_WIDTH]))
    ysn = (jnp.concatenate(parts, axis=1) * ssdg_ref[...]).astype(BF16)
    m_s = _dot(ysn, pssd_ref[...])

    k_id = lax.broadcasted_iota(jnp.int32, (x_ref.shape[1], D_FNO), 0)
    alt = jnp.where(k_id % 2 == 0, seq ** -0.5, -(seq ** -0.5))
    yf = (yf + alt * mid_ref[0:1, :]) * _silu(zf_ref[0].astype(F32))
    m_f = _dot(yf.astype(BF16), pfno_ref[...])

    heads = []
    for hd in range(MEM_HEADS):
        s = scores[hd] * (MEM_HEAD_DIM ** -0.5)
        p = jnp.exp(s - jnp.max(s, axis=-1, keepdims=True))
        denom = jnp.sum(p, axis=-1, keepdims=True)
        pv = _dot(p.astype(BF16), kv_ref[:, D_MEM + hd * MEM_HEAD_DIM:D_MEM + (hd + 1) * MEM_HEAD_DIM])
        heads.append(pv / denom)
    ym = jnp.concatenate(heads, axis=1) * _silu(zm_ref[0].astype(F32))
    m_m = _dot(ym.astype(BF16), pmem_ref[...])

    gate = _sigmoid(gates_ref[0].astype(F32))
    merged = (gate[:, :D_MODEL] * m_s + gate[:, D_MODEL:2 * D_MODEL] * m_f
              + gate[:, 2 * D_MODEL:] * m_m)
    out = x_ref[0] + _dot(merged.astype(BF16), wout_ref[...])
    if apply_final_norm:
        out = _rms_scale(out) * finalg_ref[...]
    out_ref[0] = out


def _mix(x, ys, rest, ssd_norm_g, mem, mem_norm_g, chan_dft, seq_dft, w_kv, p_ssd, p_fno, p_mem,
         w_out, layer, final_g, apply_final_norm):
    bsz, seq, d = x.shape
    rows = MIX_ROWS
    n_tiles = seq // rows
    rest3 = rest.reshape(bsz, seq, REST_WIDTH)

    def tile_spec(width, col_block=0):
        return pl.BlockSpec((1, rows, width), lambda b, t: (b, t, col_block))

    assert D_SSD == 2 * D_FNO and D_FNO == D_MEM and REST_WIDTH == 2 * N_BRANCH * D_MODEL
    in_specs = [
        tile_spec(d), tile_spec(D_SSD), tile_spec(D_SSD, 0), _const_spec((1, D_SSD)),
        pl.BlockSpec((1, seq, D_FNO), lambda b, t: (b, 0, 2)),
        tile_spec(D_FNO, 3), tile_spec(D_MEM, 4), tile_spec(D_MEM, 5),
        tile_spec(N_BRANCH * D_MODEL, 1),
        pl.BlockSpec((1, N_MEM, d), lambda b, t: (b, 0, 0), pipeline_mode=pl.Buffered(1)),
        _const_spec((1, d)), _layer_spec(w_kv, layer), _const_spec(chan_dft.shape),
        _const_spec(seq_dft.shape),
        _layer_spec(p_ssd, layer), _layer_spec(p_fno, layer), _layer_spec(p_mem, layer),
        _layer_spec(w_out, layer), _const_spec((1, d)),
    ]
    return pl.pallas_call(
        functools.partial(_mix_kernel, apply_final_norm),
        grid=(bsz, n_tiles),
        in_specs=in_specs,
        out_specs=tile_spec(d),
        out_shape=jax.ShapeDtypeStruct((bsz, seq, d), F32),
        scratch_shapes=[pltpu.VMEM((seq, D_FNO), BF16),
                        pltpu.VMEM((N_MEM, 2 * D_MEM), BF16),
                        pltpu.VMEM((seq, 2 * D_FNO), BF16),
                        pltpu.VMEM((SUBLANES, D_FNO), F32)],
        compiler_params=pltpu.CompilerParams(
            dimension_semantics=("arbitrary", "arbitrary"), vmem_limit_bytes=VMEM_LIMIT_BYTES),
        name="mix",
    )(x, ys, rest3, ssd_norm_g.reshape(1, D_SSD), rest3, rest3, rest3, rest3, rest3, mem,
      mem_norm_g.reshape(1, d), w_kv, chan_dft, seq_dft, p_ssd, p_fno, p_mem, w_out,
      final_g.reshape(1, d))


def _dft_table(n, split, sin_sign, n_cols):
    m = jnp.arange(n_cols, dtype=jnp.int32)
    k_hi = jnp.arange(n // split, dtype=jnp.int32) * split
    k_lo = jnp.arange(split, dtype=jnp.int32)
    ang_hi = (2.0 * math.pi / n) * ((k_hi[:, None] * m[None, :]) % n).astype(F32)
    ang_lo = (2.0 * math.pi / n) * ((k_lo[:, None] * m[None, :]) % n).astype(F32)
    ca, sa = jnp.cos(ang_hi), jnp.sin(ang_hi)
    cb, sb = jnp.cos(ang_lo), jnp.sin(ang_lo)
    u = jnp.concatenate([ca, sin_sign * sa], axis=1)[:, None, :]
    v = jnp.concatenate([-sa, sin_sign * ca], axis=1)[:, None, :]
    cb2 = jnp.concatenate([cb, cb], axis=1)[None, :, :]
    sb2 = jnp.concatenate([sb, sb], axis=1)[None, :, :]
    return ((u * cb2 + v * sb2) * (n ** -0.5)).astype(BF16).reshape(n, 2 * n_cols)


def _dft_tables(seq):
    return (_dft_table(FNO_GROUP_DIM, 8, 1.0, FNO_GROUP_DIM),
            _dft_table(seq, 64, -1.0, seq // 2))


def _transpose_cast_kernel(tail_rows, a_ref, b_ref, o_ref):
    a = a_ref[0]
    shifted = jnp.concatenate([a[tail_rows:], b_ref[0]], axis=0)
    rows = jnp.where(pl.program_id(1) == 0, a, shifted) if tail_rows else a
    o_ref[0] = rows.T.astype(BF16)


def _split_w_in(w_in):
    depth, d, _ = w_in.shape
    w_t = jnp.swapaxes(w_in, 1, 2)
    step = W_SPLIT_COLS
    tail = IN_SIZES[2]
    xbc_lo, dt_lo = IN_SIZES[0], IN_SIZES[0] + IN_SIZES[1]
    assert xbc_lo == step and dt_lo % step == 0 and step % tail == 0

    def call(n_steps, a_map, b_map, tail_rows, name):
        return pl.pallas_call(
            functools.partial(_transpose_cast_kernel, tail_rows),
            grid=(depth, n_steps),
            in_specs=[pl.BlockSpec((1, step, d), a_map), pl.BlockSpec((1, tail, d), b_map)],
            out_specs=pl.BlockSpec((1, d, step), lambda i, c: (i, 0, c)),
            out_shape=jax.ShapeDtypeStruct((depth, d, n_steps * step), BF16),
            compiler_params=pltpu.CompilerParams(
                dimension_semantics=("arbitrary", "arbitrary"),
                vmem_limit_bytes=VMEM_LIMIT_BYTES),
            name=name,
        )(w_t, w_t)

    first_rest = dt_lo // step
    w_xbc = call(CONV_CH // step, lambda i, c: (i, xbc_lo // step + c, 0),
                 lambda i, c: (i, 0, 0), 0, "split_w_xbc")
    w_rest = call(REST_WIDTH // step,
                  lambda i, c: (i, jnp.where(c == 0, 0, first_rest - 1 + c), 0),
                  lambda i, c: (i, (first_rest + c) * (step // tail), 0), tail, "split_w_rest")
    w_dt_t = w_t[:, dt_lo:dt_lo + tail, :].astype(BF16)
    return w_xbc, w_rest, w_dt_t


def _layer(x, mem, layer, norm_g, w_in_parts, conv_w, conv_b, dt_bias, a_log, d_skip, ssd_norm_g,
           mem_norm_g, mix_weights, final_g, tables, apply_final_norm):
    w_xbc, w_rest, w_dtT = w_in_parts
    rest, xsT, bc, dtT = _in_proj(x, norm_g, w_xbc, w_rest, w_dtT, layer, conv_w, conv_b)
    ys = _ssd(xsT, bc, dtT, dt_bias, a_log, d_skip)
    return _mix(x, ys, rest, ssd_norm_g, mem, mem_norm_g, *tables, *mix_weights, layer, final_g,
                apply_final_norm)


def kernel(x, mem, norm_g, w_in, conv_w, conv_b, dt_bias, a_log, d_skip, ssd_norm_g, mem_norm_g,
           w_kv, p_ssd, p_fno, p_mem, w_out, final_g):
    depth = norm_g.shape[0]
    tables = _dft_tables(x.shape[1])
    w_in_parts = _split_w_in(w_in)
    mix_weights = tuple(w.astype(BF16) for w in (w_kv, p_ssd, p_fno, p_mem, w_out))
    for i in range(depth):
        x = _layer(x, mem, i, norm_g[i], w_in_parts, conv_w[i], conv_b[i], dt_bias[i], a_log[i],
                   d_skip[i], ssd_norm_g[i], mem_norm_g[i], mix_weights, final_g, tables,
                   i == depth - 1)
    return x
```

```python
import functools
import math

import jax
import jax.numpy as jnp
from jax import lax
from jax.experimental import pallas as pl
from jax.experimental.pallas import tpu as pltpu

F32 = jnp.float32
BF16 = jnp.bfloat16
HIGHEST = lax.Precision.HIGHEST

D_MODEL = 1024
D_SSD = 1024
SSD_HEAD_DIM = 64
SSD_HEADS = D_SSD // SSD_HEAD_DIM
SSD_GROUPS = 4
HEADS_PER_GROUP = SSD_HEADS // SSD_GROUPS
GROUP_WIDTH = HEADS_PER_GROUP * SSD_HEAD_DIM
SSD_STATE = 128
BC_WIDTH = SSD_GROUPS * SSD_STATE
D_CONV = 5
CONV_CH = D_SSD + 2 * BC_WIDTH
CHUNK = 128
D_FNO = 512
FNO_GROUPS = 4
FNO_GROUP_DIM = D_FNO // FNO_GROUPS
N_MEM = 256
MEM_HEADS = 4
D_MEM = 512
MEM_HEAD_DIM = D_MEM // MEM_HEADS
N_BRANCH = 3
EPS = 1e-6
IN_SIZES = (D_SSD, CONV_CH, 2 * SSD_HEADS, D_FNO, D_FNO, D_MEM, D_MEM, N_BRANCH * D_MODEL)
REST_WIDTH = sum(IN_SIZES) - CONV_CH - 2 * SSD_HEADS
XBC_STEP = 256
REST_STEP = REST_WIDTH // (CONV_CH // XBC_STEP)

SUBLANES = 8
HALO_ROWS = 16
CONV_STRIDE = 4
CONV_STRIP = 128
CONV_HALO = D_CONV // 2
IN_PROJ_ROWS = 512
MIX_ROWS = 512
W_SPLIT_COLS = 1024
FWD_UNROLL = 8
BWD_UNROLL = 8
VMEM_LIMIT_BYTES = 56 * 1024 * 1024


def _sigmoid(v):
    return 0.5 * jnp.tanh(0.5 * v) + 0.5


def _silu(v):
    h = 0.5 * v
    return h * (jnp.tanh(h) + 1.0)


def _softplus(v):
    return jnp.maximum(v, 0.0) + jnp.log1p(jnp.exp(-jnp.abs(v)))


def _rms_scale(v):
    return v * lax.rsqrt(jnp.mean(v * v, axis=-1, keepdims=True) + EPS)


def _dot(a, b):
    return jnp.dot(a, b, preferred_element_type=F32)


def _dot_nt(a, b):
    return lax.dot_general(a, b, (((1,), (1,)), ((), ())), preferred_element_type=F32)


def _dot_exact(a, b):
    return jnp.dot(a, b, precision=HIGHEST, preferred_element_type=F32)


def _in_proj_kernel(tiles_per_seq,
                    x_ref, xprev_ref, xnext_ref, g_ref,
                    wxbc_ref, wrest_ref, wdtT_ref, convw_ref, convb_ref,
                    rest_ref, xsT_ref, bc_ref, dtT_ref,
                    stage_ref, act_ref, h_ref):
    rows = x_ref.shape[0]
    lt = pl.program_id(0) % tiles_per_seq
    g = g_ref[...]
    keep_prev = (lt > 0).astype(F32)
    keep_next = (lt < tiles_per_seq - 1).astype(F32)
    h_ref[0:HALO_ROWS, :] = (_rms_scale(xprev_ref[...]) * g * keep_prev).astype(BF16)
    h_ref[HALO_ROWS:HALO_ROWS + rows, :] = (_rms_scale(x_ref[...]) * g).astype(BF16)
    h_ref[HALO_ROWS + rows:, :] = (_rms_scale(xnext_ref[...]) * g * keep_next).astype(BF16)
    h_own = h_ref.at[HALO_ROWS:HALO_ROWS + rows, :]

    strips_per_step = XBC_STEP // CONV_STRIP

    def project_xbc(j):
        res = _dot(h_ref[...], wxbc_ref[:, pl.ds(j * XBC_STEP, XBC_STEP)])
        for t in range(strips_per_step):
            stage_ref[j * strips_per_step + t] = res[:, t * CONV_STRIP:(t + 1) * CONV_STRIP]

    def project_rest(j):
        cols = pl.ds(j * REST_STEP, REST_STEP)
        rest_ref[:, cols] = _dot(h_own[...], wrest_ref[:, cols]).astype(BF16)

    def conv(j, is_x):
        block = SUBLANES * CONV_STRIDE
        for t in range(strips_per_step):
            slab = j * strips_per_step + t
            cs = pl.ds(slab * CONV_STRIP, CONV_STRIP)
            bias = convb_ref[:, cs]
            taps = [convw_ref[k:k + 1, cs] for k in range(D_CONV)]
            for r0 in range(0, rows, block):
                base = HALO_ROWS + r0 - CONV_HALO
                regs = [stage_ref[slab, pl.ds(base + m, SUBLANES, stride=CONV_STRIDE), :]
                        for m in range(CONV_STRIDE + 2 * CONV_HALO)]
                for i in range(CONV_STRIDE):
                    acc = bias + taps[0] * regs[i]
                    for k in range(1, D_CONV):
                        acc = acc + taps[k] * regs[i + k]
                    act_ref[t, pl.ds(r0 + i, SUBLANES, stride=CONV_STRIDE), :] = _silu(acc)
            if is_x:
                for r0 in range(0, rows, CHUNK):
                    xsT_ref[0, cs, r0:r0 + CHUNK] = act_ref[t, r0:r0 + CHUNK, :].T.astype(BF16)
            else:
                bc_ref[:, pl.ds(slab * CONV_STRIP - D_SSD, CONV_STRIP)] = act_ref[t].astype(BF16)

    project_xbc(0)
    dtT_ref[...] = _dot_nt(wdtT_ref[...], h_own[...])

    n_steps = CONV_CH // XBC_STEP
    for j in range(n_steps):
        conv(j, j * XBC_STEP < D_SSD)
        if j + 1 < n_steps:
            project_xbc(j + 1)
        project_rest(j)


def _const_spec(shape):
    nd = len(shape)
    return pl.BlockSpec(shape, lambda *_: (0,) * nd, pipeline_mode=pl.Buffered(1))


def _layer_spec(stacked, layer):
    return pl.BlockSpec((None,) + stacked.shape[1:], lambda *_: (layer, 0, 0),
                        pipeline_mode=pl.Buffered(1))


def _in_proj(x, norm_g, w_xbc, w_rest, w_dtT, layer, conv_w, conv_b):
    bsz, seq, d = x.shape
    m = bsz * seq
    rows = IN_PROJ_ROWS
    tiles_per_seq = seq // rows
    n_tiles = m // rows
    halo_per_tile = rows // HALO_ROWS
    n_halo = m // HALO_ROWS
    x2 = x.reshape(m, d)

    def row_spec(width):
        return pl.BlockSpec((rows, width), lambda i: (i, 0))

    in_specs = [
        row_spec(d),
        pl.BlockSpec((HALO_ROWS, d), lambda i: (jnp.maximum(i * halo_per_tile - 1, 0), 0)),
        pl.BlockSpec((HALO_ROWS, d),
                     lambda i: (jnp.minimum((i + 1) * halo_per_tile, n_halo - 1), 0)),
        _const_spec((1, d)),
        _layer_spec(w_xbc, layer), _layer_spec(w_rest, layer), _layer_spec(w_dtT, layer),
        _const_spec(conv_w.shape), _const_spec((1, CONV_CH)),
    ]
    out_shape = [
        jax.ShapeDtypeStruct((m, REST_WIDTH), BF16),
        jax.ShapeDtypeStruct((bsz, D_SSD, seq), BF16),
        jax.ShapeDtypeStruct((m, 2 * BC_WIDTH), BF16),
        jax.ShapeDtypeStruct((2 * SSD_HEADS, m), F32),
    ]
    out_specs = [
        row_spec(REST_WIDTH),
        pl.BlockSpec((1, D_SSD, rows), lambda i: (i // tiles_per_seq, 0, i % tiles_per_seq)),
        row_spec(2 * BC_WIDTH),
        pl.BlockSpec((2 * SSD_HEADS, rows), lambda i: (0, i)),
    ]
    return pl.pallas_call(
        functools.partial(_in_proj_kernel, tiles_per_seq),
        grid=(n_tiles,),
        in_specs=in_specs,
        out_specs=out_specs,
        out_shape=out_shape,
        scratch_shapes=[pltpu.VMEM((CONV_CH // CONV_STRIP, rows + 2 * HALO_ROWS, CONV_STRIP), F32),
                        pltpu.VMEM((XBC_STEP // CONV_STRIP, rows, CONV_STRIP), F32),
                        pltpu.VMEM((rows + 2 * HALO_ROWS, d), BF16)],
        compiler_params=pltpu.CompilerParams(
            dimension_semantics=("arbitrary",), vmem_limit_bytes=VMEM_LIMIT_BYTES),
        name="in_proj",
    )(x2, x2, x2, norm_g.reshape(1, d), w_xbc, w_rest, w_dtT, conv_w, conv_b.reshape(1, CONV_CH))


def _expand_rows(v):
    lanes = v.shape[1]
    return jnp.concatenate(
        [jnp.broadcast_to(v[j:j + 1, :], (SSD_HEAD_DIM, lanes)) for j in range(v.shape[0])], axis=0)


TAB_QF, TAB_QB, TAB_EF, TAB_EB, TAB_WF, TAB_WB, TAB_DECAY_F, TAB_DECAY_B, TAB_DTB = range(9)
LOG2E = 1.4426950408889634


def _ssd_kernel(n_chunks,
                xsT_ref, b_ref, c_ref, dtT_ref, bias_col_ref, alog_col_ref, dskip_ref,
                y_ref,
                snap_ref, state_ref, tab_ref, colt_ref):
    H = SSD_HEADS
    row_id = lax.broadcasted_iota(jnp.int32, (CHUNK, CHUNK), 0)
    col_id = lax.broadcasted_iota(jnp.int32, (CHUNK, CHUNK), 1)
    lower = row_id >= col_id
    diag = row_id == col_id
    lower_f = lower.astype(F32)
    upper_f = (row_id <= col_id).astype(F32)

    dt_all = _softplus(dtT_ref[...] + bias_col_ref[...])
    a_all = dt_all * -jnp.exp(alog_col_ref[...])

    def stack(v, lo):
        return jnp.concatenate(
            [v[lo:lo + H, c * CHUNK:(c + 1) * CHUNK] for c in range(n_chunks)], axis=0)

    dt_f, dt_b = stack(dt_all, 0), stack(dt_all, H)
    cum_f = _dot_exact(stack(a_all, 0), upper_f)
    cum_b = _dot_exact(stack(a_all, H), lower_f)
    total_f = jnp.broadcast_to(cum_f[:, CHUNK - 1:CHUNK], cum_f.shape)
    total_b = jnp.broadcast_to(cum_b[:, 0:1], cum_b.shape)
    p_f = cum_f * LOG2E
    p_b = cum_b * LOG2E
    tab_ref[TAB_QF] = p_f - jnp.log2(dt_f)
    tab_ref[TAB_QB] = p_b - jnp.log2(dt_b)
    tab_ref[TAB_EF] = jnp.exp(cum_f)
    tab_ref[TAB_EB] = jnp.exp(cum_b)
    tab_ref[TAB_WF] = dt_f * jnp.exp(total_f - cum_f)
    tab_ref[TAB_WB] = dt_b * jnp.exp(total_b - cum_b)
    tab_ref[TAB_DECAY_F] = jnp.exp(total_f)
    tab_ref[TAB_DECAY_B] = jnp.exp(total_b)
    tab_ref[TAB_DTB] = dt_b
    pad = jnp.zeros((CHUNK - 2 * H, CHUNK), F32)
    for c in range(n_chunks):
        rows = slice(c * H, (c + 1) * H)
        colt_ref[c] = jnp.concatenate([p_f[rows], p_b[rows], pad], axis=0).T

    def chunk_rows(k, c):
        return tab_ref[k, pl.ds(pl.multiple_of(c * H, H), H), :]

    def state_update(c, g, xs_t, w, decay):
        hs = slice(g * HEADS_PER_GROUP, (g + 1) * HEADS_PER_GROUP)
        rs = slice(g * GROUP_WIDTH, (g + 1) * GROUP_WIDTH)
        xd_t = (xs_t * _expand_rows(w[hs])).astype(BF16)
        b_g = b_ref[0, pl.ds(c * CHUNK, CHUNK), g * SSD_STATE:(g + 1) * SSD_STATE]
        state_ref[rs, :] = _expand_rows(decay[hs]) * state_ref[rs, :] + _dot(xd_t, b_g)

    state_ref[...] = jnp.zeros_like(state_ref)

    def fwd_body(c, carry):
        snap_ref[c] = state_ref[...].astype(BF16)
        w, decay = chunk_rows(TAB_WF, c), chunk_rows(TAB_DECAY_F, c)
        for g in range(SSD_GROUPS):
            rs = slice(g * GROUP_WIDTH, (g + 1) * GROUP_WIDTH)
            xs_t = xsT_ref[0, rs, pl.ds(c * CHUNK, CHUNK)].astype(F32)
            state_update(c, g, xs_t, w, decay)
        return carry

    lax.fori_loop(0, n_chunks, fwd_body, 0, unroll=math.gcd(n_chunks, FWD_UNROLL))

    state_ref[...] = jnp.zeros_like(state_ref)

    def bwd_body(it, carry):
        c = n_chunks - 1 - it
        q_f, q_b = chunk_rows(TAB_QF, c), chunk_rows(TAB_QB, c)
        e_f, e_b = chunk_rows(TAB_EF, c), chunk_rows(TAB_EB, c)
        w_b, decay_b = chunk_rows(TAB_WB, c), chunk_rows(TAB_DECAY_B, c)
        dt_b_c = chunk_rows(TAB_DTB, c)
        groups = []
        for g in range(SSD_GROUPS):
            hs = slice(g * HEADS_PER_GROUP, (g + 1) * HEADS_PER_GROUP)
            rs = slice(g * GROUP_WIDTH, (g + 1) * GROUP_WIDTH)
            b_g = b_ref[0, pl.ds(c * CHUNK, CHUNK), g * SSD_STATE:(g + 1) * SSD_STATE]
            c_g = c_ref[0, pl.ds(c * CHUNK, CHUNK), g * SSD_STATE:(g + 1) * SSD_STATE]
            xs_bf = xsT_ref[0, rs, pl.ds(c * CHUNK, CHUNK)]
            xs_t = xs_bf.astype(F32)
            cb = _dot_nt(c_g, b_g)
            states = jnp.concatenate([snap_ref[c, rs, :], state_ref[rs, :].astype(BF16)], axis=0)
            y_off = _dot_nt(states, c_g)
            state_update(c, g, xs_t, w_b, decay_b)
            groups.append((hs, rs, xs_bf, xs_t, cb, y_off))
        for g, (hs, rs, xs_bf, xs_t, cb, y_off) in enumerate(groups):
            y_parts = []
            for j in range(HEADS_PER_GROUP):
                hd = g * HEADS_PER_GROUP + j
                arg = jnp.where(lower,
                                colt_ref[c, :, hd:hd + 1] - q_f[hd:hd + 1, :],
                                colt_ref[c, :, H + hd:H + hd + 1] - q_b[hd:hd + 1, :])
                m_h = (cb * jnp.exp2(arg)).astype(BF16)
                y_parts.append(_dot_nt(xs_bf[j * SSD_HEAD_DIM:(j + 1) * SSD_HEAD_DIM, :], m_h))
            cb_diag = jnp.sum(jnp.where(diag, cb, 0.0), axis=0, keepdims=True)
            own = dskip_ref[rs, :] + _expand_rows(cb_diag * dt_b_c[hs])
            y_t = (jnp.concatenate(y_parts, axis=0)
                   + y_off[:GROUP_WIDTH] * _expand_rows(e_f[hs])
                   + y_off[GROUP_WIDTH:] * _expand_rows(e_b[hs])
                   + own * xs_t)
            y_ref[0, pl.ds(c * CHUNK, CHUNK), rs] = y_t.T.astype(BF16)
        return carry

    lax.fori_loop(0, n_chunks, bwd_body, 0, unroll=math.gcd(n_chunks, BWD_UNROLL))


def _ssd(xsT, bc, dtT, dt_bias, a_log, d_skip):
    bsz, _, seq = xsT.shape
    n_chunks = seq // CHUNK
    two_h = 2 * SSD_HEADS
    bias_col = jnp.broadcast_to(dt_bias.reshape(two_h, 1).astype(F32), (two_h, seq))
    alog_col = jnp.broadcast_to(a_log.reshape(two_h, 1).astype(F32), (two_h, seq))
    dskip_col = jnp.broadcast_to(
        jnp.repeat(d_skip.astype(F32), SSD_HEAD_DIM)[:, None], (D_SSD, CHUNK))
    bc3 = bc.reshape(bsz, seq, 2 * BC_WIDTH)
    return pl.pallas_call(
        functools.partial(_ssd_kernel, n_chunks),
        grid=(bsz,),
        in_specs=[
            pl.BlockSpec((1, D_SSD, seq), lambda b: (b, 0, 0)),
            pl.BlockSpec((1, seq, BC_WIDTH), lambda b: (b, 0, 0)),
            pl.BlockSpec((1, seq, BC_WIDTH), lambda b: (b, 0, 1)),
            pl.BlockSpec((two_h, seq), lambda b: (0, b)),
            _const_spec((two_h, seq)), _const_spec((two_h, seq)),
            _const_spec((D_SSD, CHUNK)),
        ],
        out_specs=pl.BlockSpec((1, seq, D_SSD), lambda b: (b, 0, 0)),
        out_shape=jax.ShapeDtypeStruct((bsz, seq, D_SSD), BF16),
        scratch_shapes=[pltpu.VMEM((n_chunks, D_SSD, SSD_STATE), BF16),
                        pltpu.VMEM((D_SSD, SSD_STATE), F32),
                        pltpu.VMEM((TAB_DTB + 1, n_chunks * SSD_HEADS, CHUNK), F32),
                        pltpu.VMEM((n_chunks, CHUNK, CHUNK), F32)],
        compiler_params=pltpu.CompilerParams(
            dimension_semantics=("arbitrary",), vmem_limit_bytes=VMEM_LIMIT_BYTES),
        name="ssd",
    )(xsT, bc3, bc3, dtT, bias_col, alog_col, dskip_col)


def _mix_kernel(apply_final_norm,
                x_ref, ys_ref, rest_ref, ssdg_ref, uf_ref,
                mem_ref, memg_ref, wkv_ref, chan_dft_ref, seq_dft_ref,
                pssd_ref, pfno_ref, pmem_ref, wout_ref, finalg_ref,
                out_ref,
                ab_ref, kv_ref, dft_ref, mid_ref):
    seq = uf_ref.shape[1]
    lt = pl.program_id(1)
    zf_lo = D_SSD + D_FNO
    q_lo = zf_lo + D_FNO
    zm_lo = q_lo + D_MEM
    gates_lo = zm_lo + D_MEM

    @pl.when(lt == 0)
    def _per_sequence():
        half = seq // 2
        r_id = lax.broadcasted_iota(jnp.int32, (CHUNK, 2 * CHUNK), 0)
        c_id = lax.broadcasted_iota(jnp.int32, (CHUNK, 2 * CHUNK), 1)
        reverse = (r_id + c_id == CHUNK).astype(BF16)
        sign = jnp.concatenate([jnp.ones((1, D_FNO), F32), -jnp.ones((1, D_FNO), F32)], axis=1)
        for gi in range(FNO_GROUPS):
            cs = slice(gi * FNO_GROUP_DIM, (gi + 1) * FNO_GROUP_DIM)
            t = _dot(uf_ref[0, :, cs], chan_dft_ref[...]).astype(BF16)
            dft_ref[:, cs] = t[:, :FNO_GROUP_DIM]
            dft_ref[:, D_FNO + gi * FNO_GROUP_DIM:D_FNO + (gi + 1) * FNO_GROUP_DIM] = (
                t[:, FNO_GROUP_DIM:])
        mid_ref[0:1, :] = dft_ref[half:half + 1, :D_FNO].astype(F32)
        for blk in range(half // CHUNK):
            lo = seq - (blk + 1) * CHUNK
            if blk == 0:
                window = jnp.concatenate(
                    [dft_ref[lo:, :], jnp.zeros((CHUNK, 2 * D_FNO), BF16)], axis=0)
            else:
                window = dft_ref[lo:lo + 2 * CHUNK, :]
            own = dft_ref[blk * CHUNK:(blk + 1) * CHUNK, :].astype(F32)
            folded = (own + sign * _dot(reverse, window)).astype(BF16)
            ab_ref[blk * CHUNK:(blk + 1) * CHUNK, :] = folded[:, :D_FNO]
            ab_ref[half + blk * CHUNK:half + (blk + 1) * CHUNK, :] = folded[:, D_FNO:]
        hm = (_rms_scale(mem_ref[0]) * memg_ref[...]).astype(BF16)
        kv_ref[...] = _dot(hm, wkv_ref[...]).astype(BF16)

    yf = _dot(seq_dft_ref[...], ab_ref[...])
    q = rest_ref[0, :, q_lo:zm_lo]
    scores = []
    for hd in range(MEM_HEADS):
        cs = slice(hd * MEM_HEAD_DIM, (hd + 1) * MEM_HEAD_DIM)
        scores.append(_dot_nt(q[:, cs], kv_ref[:, cs]))

    t = ys_ref[0].astype(F32) * _silu(rest_ref[0, :, :D_SSD].astype(F32))
    parts = []
    for gi in range(SSD_GROUPS):
        parts.append(_rms_scale(t[:, gi * GROUP_WIDTH:(gi + 1) * GROUP_WIDTH]))
    ysn = (jnp.concatenate(parts, axis=1) * ssdg_ref[...]).astype(BF16)
    m_s = _dot(ysn, pssd_ref[...])

    k_id = lax.broadcasted_iota(jnp.int32, (x_ref.shape[1], D_FNO), 0)
    alt = jnp.where(k_id % 2 == 0, seq ** -0.5, -(seq ** -0.5))
    yf = (yf + alt * mid_ref[0:1, :]) * _silu(rest_ref[0, :, zf_lo:q_lo].astype(F32))
    m_f = _dot(yf.astype(BF16), pfno_ref[...])

    heads = []
    for hd in range(MEM_HEADS):
        s = scores[hd] * (MEM_HEAD_DIM ** -0.5)
        p = jnp.exp(s - jnp.max(s, axis=-1, keepdims=True))
        denom = jnp.sum(p, axis=-1, keepdims=True)
        pv = _dot(p.astype(BF16), kv_ref[:, D_MEM + hd * MEM_HEAD_DIM:D_MEM + (hd + 1) * MEM_HEAD_DIM])
        heads.append(pv / denom)
    ym = jnp.concatenate(heads, axis=1) * _silu(rest_ref[0, :, zm_lo:gates_lo].astype(F32))
    m_m = _dot(ym.astype(BF16), pmem_ref[...])

    gate = _sigmoid(rest_ref[0, :, gates_lo:].astype(F32))
    merged = (gate[:, :D_MODEL] * m_s + gate[:, D_MODEL:2 * D_MODEL] * m_f
              + gate[:, 2 * D_MODEL:] * m_m)
    out = x_ref[0] + _dot(merged.astype(BF16), wout_ref[...])
    if apply_final_norm:
        out = _rms_scale(out) * finalg_ref[...]
    out_ref[0] = out


def _mix(x, ys, rest, ssd_norm_g, mem, mem_norm_g, chan_dft, seq_dft, w_kv, p_ssd, p_fno, p_mem,
         w_out, layer, final_g, apply_final_norm):
    bsz, seq, d = x.shape
    rows = MIX_ROWS
    n_tiles = seq // rows
    rest3 = rest.reshape(bsz, seq, REST_WIDTH)

    def tile_spec(width):
        return pl.BlockSpec((1, rows, width), lambda b, t: (b, t, 0))

    assert D_SSD == 2 * D_FNO
    in_specs = [
        tile_spec(d), tile_spec(D_SSD), tile_spec(REST_WIDTH), _const_spec((1, D_SSD)),
        pl.BlockSpec((1, seq, D_FNO), lambda b, t: (b, 0, 2)),
        pl.BlockSpec((1, N_MEM, d), lambda b, t: (b, 0, 0)),
        _const_spec((1, d)), _layer_spec(w_kv, layer), _const_spec(chan_dft.shape),
        pl.BlockSpec((rows, seq), lambda b, t: (t, 0)),
        _layer_spec(p_ssd, layer), _layer_spec(p_fno, layer), _layer_spec(p_mem, layer),
        _layer_spec(w_out, layer), _const_spec((1, d)),
    ]
    return pl.pallas_call(
        functools.partial(_mix_kernel, apply_final_norm),
        grid=(bsz, n_tiles),
        in_specs=in_specs,
        out_specs=tile_spec(d),
        out_shape=jax.ShapeDtypeStruct((bsz, seq, d), F32),
        scratch_shapes=[pltpu.VMEM((seq, D_FNO), BF16),
                        pltpu.VMEM((N_MEM, 2 * D_MEM), BF16),
                        pltpu.VMEM((seq, 2 * D_FNO), BF16),
                        pltpu.VMEM((SUBLANES, D_FNO), F32)],
        compiler_params=pltpu.CompilerParams(
            dimension_semantics=("arbitrary", "arbitrary"), vmem_limit_bytes=VMEM_LIMIT_BYTES),
        name="mix",
    )(x, ys, rest3, ssd_norm_g.reshape(1, D_SSD), rest3, mem,
      mem_norm_g.reshape(1, d), w_kv, chan_dft, seq_dft, p_ssd, p_fno, p_mem, w_out,
      final_g.reshape(1, d))


def _dft_table(n, split, sin_sign, n_cols):
    m = jnp.arange(n_cols, dtype=jnp.int32)
    k_hi = jnp.arange(n // split, dtype=jnp.int32) * split
    k_lo = jnp.arange(split, dtype=jnp.int32)
    ang_hi = (2.0 * math.pi / n) * ((k_hi[:, None] * m[None, :]) % n).astype(F32)
    ang_lo = (2.0 * math.pi / n) * ((k_lo[:, None] * m[None, :]) % n).astype(F32)
    ca, sa = jnp.cos(ang_hi), jnp.sin(ang_hi)
    cb, sb = jnp.cos(ang_lo), jnp.sin(ang_lo)
    u = jnp.concatenate([ca, sin_sign * sa], axis=1)[:, None, :]
    v = jnp.concatenate([-sa, sin_sign * ca], axis=1)[:, None, :]
    cb2 = jnp.concatenate([cb, cb], axis=1)[None, :, :]
    sb2 = jnp.concatenate([sb, sb], axis=1)[None, :, :]
    return ((u * cb2 + v * sb2) * (n ** -0.5)).astype(BF16).reshape(n, 2 * n_cols)


def _dft_tables(seq):
    return (_dft_table(FNO_GROUP_DIM, 8, 1.0, FNO_GROUP_DIM),
            _dft_table(seq, 64, -1.0, seq // 2))


def _transpose_cast_kernel(tail_rows, a_ref, b_ref, o_ref):
    a = a_ref[0]
    shifted = jnp.concatenate([a[tail_rows:], b_ref[0]], axis=0)
    rows = jnp.where(pl.program_id(1) == 0, a, shifted) if tail_rows else a
    o_ref[0] = rows.T.astype(BF16)


def _split_w_in(w_in):
    depth, d, _ = w_in.shape
    w_t = jnp.swapaxes(w_in, 1, 2)
    step = W_SPLIT_COLS
    tail = IN_SIZES[2]
    xbc_lo, dt_lo = IN_SIZES[0], IN_SIZES[0] + IN_SIZES[1]
    assert xbc_lo == step and dt_lo % step == 0 and step % tail == 0

    def call(n_steps, a_map, b_map, tail_rows, name):
        return pl.pallas_call(
            functools.partial(_transpose_cast_kernel, tail_rows),
            grid=(depth, n_steps),
            in_specs=[pl.BlockSpec((1, step, d), a_map), pl.BlockSpec((1, tail, d), b_map)],
            out_specs=pl.BlockSpec((1, d, step), lambda i, c: (i, 0, c)),
            out_shape=jax.ShapeDtypeStruct((depth, d, n_steps * step), BF16),
            compiler_params=pltpu.CompilerParams(
                dimension_semantics=("arbitrary", "arbitrary"),
                vmem_limit_bytes=VMEM_LIMIT_BYTES),
            name=name,
        )(w_t, w_t)

    first_rest = dt_lo // step
    w_xbc = call(CONV_CH // step, lambda i, c: (i, xbc_lo // step + c, 0),
                 lambda i, c: (i, 0, 0), 0, "split_w_xbc")
    w_rest = call(REST_WIDTH // step,
                  lambda i, c: (i, jnp.where(c == 0, 0, first_rest - 1 + c), 0),
                  lambda i, c: (i, (first_rest + c) * (step // tail), 0), tail, "split_w_rest")
    w_dt_t = w_t[:, dt_lo:dt_lo + tail, :].astype(BF16)
    return w_xbc, w_rest, w_dt_t


def _layer(x, mem, layer, norm_g, w_in_parts, conv_w, conv_b, dt_bias, a_log, d_skip, ssd_norm_g,
           mem_norm_g, mix_weights, final_g, tables, apply_final_norm):
    w_xbc, w_rest, w_dtT = w_in_parts
    rest, xsT, bc, dtT = _in_proj(x, norm_g, w_xbc, w_rest, w_dtT, layer, conv_w, conv_b)
    ys = _ssd(xsT, bc, dtT, dt_bias, a_log, d_skip)
    return _mix(x, ys, rest, ssd_norm_g, mem, mem_norm_g, *tables, *mix_weights, layer, final_g,
                apply_final_norm)


def kernel(x, mem, norm_g, w_in, conv_w, conv_b, dt_bias, a_log, d_skip, ssd_norm_g, mem_norm_g,
           w_kv, p_ssd, p_fno, p_mem, w_out, final_g):
    depth = norm_g.shape[0]
    tables = _dft_tables(x.shape[1])
    w_in_parts = _split_w_in(w_in)
    mix_weights = tuple(w.astype(BF16) for w in (w_kv, p_ssd, p_fno, p_mem, w_out))
    for i in range(depth):
        x = _layer(x, mem, i, norm_g[i], w_in_parts, conv_w[i], conv_b[i], dt_bias[i], a_log[i],
                   d_skip[i], ssd_norm_g[i], mem_norm_g[i], mix_weights, final_g, tables,
                   i == depth - 1)
    return x
```

```python
import functools
import math

import jax
import jax.numpy as jnp
from jax import lax
from jax.experimental import pallas as pl
from jax.experimental.pallas import tpu as pltpu

F32 = jnp.float32
BF16 = jnp.bfloat16
HIGHEST = lax.Precision.HIGHEST

D_MODEL = 1024
D_SSD = 1024
SSD_HEAD_DIM = 64
SSD_HEADS = D_SSD // SSD_HEAD_DIM
SSD_GROUPS = 4
HEADS_PER_GROUP = SSD_HEADS // SSD_GROUPS
GROUP_WIDTH = HEADS_PER_GROUP * SSD_HEAD_DIM
SSD_STATE = 128
BC_WIDTH = SSD_GROUPS * SSD_STATE
D_CONV = 5
CONV_CH = D_SSD + 2 * BC_WIDTH
CHUNK = 128
D_FNO = 512
FNO_GROUPS = 4
FNO_GROUP_DIM = D_FNO // FNO_GROUPS
N_MEM = 256
MEM_HEADS = 4
D_MEM = 512
MEM_HEAD_DIM = D_MEM // MEM_HEADS
N_BRANCH = 3
EPS = 1e-6
IN_SIZES = (D_SSD, CONV_CH, 2 * SSD_HEADS, D_FNO, D_FNO, D_MEM, D_MEM, N_BRANCH * D_MODEL)
REST_WIDTH = sum(IN_SIZES) - CONV_CH - 2 * SSD_HEADS
XBC_STEP = 256
REST_STEP = REST_WIDTH // (CONV_CH // XBC_STEP)

SUBLANES = 8
HALO_ROWS = 16
CONV_STRIDE = 4
CONV_STRIP = 128
CONV_HALO = D_CONV // 2
IN_PROJ_ROWS = 512
MIX_ROWS = 512
W_SPLIT_COLS = 1024
FWD_UNROLL = 16
BWD_UNROLL = 16
VMEM_LIMIT_BYTES = 56 * 1024 * 1024


def _sigmoid(v):
    return 0.5 * jnp.tanh(0.5 * v) + 0.5


def _silu(v):
    h = 0.5 * v
    return h * (jnp.tanh(h) + 1.0)


def _softplus(v):
    return jnp.maximum(v, 0.0) + jnp.log1p(jnp.exp(-jnp.abs(v)))


def _rms_scale(v):
    return v * lax.rsqrt(jnp.mean(v * v, axis=-1, keepdims=True) + EPS)


def _dot(a, b):
    return jnp.dot(a, b, preferred_element_type=F32)


def _dot_nt(a, b):
    return lax.dot_general(a, b, (((1,), (1,)), ((), ())), preferred_element_type=F32)


def _dot_exact(a, b):
    return jnp.dot(a, b, precision=HIGHEST, preferred_element_type=F32)


def _in_proj_kernel(tiles_per_seq,
                    x_ref, xprev_ref, xnext_ref, g_ref,
                    wxbc_ref, wrest_ref, wdtT_ref, convw_ref, convb_ref,
                    rest_ref, xsT_ref, bc_ref, dtT_ref,
                    stage_ref, act_ref, h_ref):
    rows = x_ref.shape[0]
    lt = pl.program_id(0) % tiles_per_seq
    g = g_ref[...]
    keep_prev = (lt > 0).astype(F32)
    keep_next = (lt < tiles_per_seq - 1).astype(F32)
    h_ref[0:HALO_ROWS, :] = (_rms_scale(xprev_ref[...]) * g * keep_prev).astype(BF16)
    h_ref[HALO_ROWS:HALO_ROWS + rows, :] = (_rms_scale(x_ref[...]) * g).astype(BF16)
    h_ref[HALO_ROWS + rows:, :] = (_rms_scale(xnext_ref[...]) * g * keep_next).astype(BF16)
    h_own = h_ref.at[HALO_ROWS:HALO_ROWS + rows, :]

    strips_per_step = XBC_STEP // CONV_STRIP

    def project_xbc(j):
        res = _dot(h_ref[...], wxbc_ref[:, pl.ds(j * XBC_STEP, XBC_STEP)])
        for t in range(strips_per_step):
            stage_ref[j * strips_per_step + t] = res[:, t * CONV_STRIP:(t + 1) * CONV_STRIP]

    def project_rest(j):
        cols = pl.ds(j * REST_STEP, REST_STEP)
        rest_ref[:, cols] = _dot(h_own[...], wrest_ref[:, cols]).astype(BF16)

    def conv(j, is_x):
        block = SUBLANES * CONV_STRIDE
        for t in range(strips_per_step):
            slab = j * strips_per_step + t
            cs = pl.ds(slab * CONV_STRIP, CONV_STRIP)
            bias = convb_ref[:, cs]
            taps = [convw_ref[k:k + 1, cs] for k in range(D_CONV)]
            for r0 in range(0, rows, block):
                base = HALO_ROWS + r0 - CONV_HALO
                regs = [stage_ref[slab, pl.ds(base + m, SUBLANES, stride=CONV_STRIDE), :]
                        for m in range(CONV_STRIDE + 2 * CONV_HALO)]
                for i in range(CONV_STRIDE):
                    acc = bias + taps[0] * regs[i]
                    for k in range(1, D_CONV):
                        acc = acc + taps[k] * regs[i + k]
                    act_ref[t, pl.ds(r0 + i, SUBLANES, stride=CONV_STRIDE), :] = _silu(acc)
            if is_x:
                for r0 in range(0, rows, CHUNK):
                    xsT_ref[0, cs, r0:r0 + CHUNK] = act_ref[t, r0:r0 + CHUNK, :].T.astype(BF16)
            else:
                bc_ref[:, pl.ds(slab * CONV_STRIP - D_SSD, CONV_STRIP)] = act_ref[t].astype(BF16)

    project_xbc(0)
    dtT_ref[...] = _dot_nt(wdtT_ref[...], h_own[...])

    n_steps = CONV_CH // XBC_STEP
    for j in range(n_steps):
        conv(j, j * XBC_STEP < D_SSD)
        if j + 1 < n_steps:
            project_xbc(j + 1)
        project_rest(j)


def _const_spec(shape):
    nd = len(shape)
    return pl.BlockSpec(shape, lambda *_: (0,) * nd, pipeline_mode=pl.Buffered(1))


def _layer_spec(stacked, layer):
    return pl.BlockSpec((None,) + stacked.shape[1:], lambda *_: (layer, 0, 0),
                        pipeline_mode=pl.Buffered(1))


def _in_proj(x, norm_g, w_xbc, w_rest, w_dtT, layer, conv_w, conv_b):
    bsz, seq, d = x.shape
    m = bsz * seq
    rows = IN_PROJ_ROWS
    tiles_per_seq = seq // rows
    n_tiles = m // rows
    halo_per_tile = rows // HALO_ROWS
    n_halo = m // HALO_ROWS
    x2 = x.reshape(m, d)

    def row_spec(width):
        return pl.BlockSpec((rows, width), lambda i: (i, 0))

    in_specs = [
        row_spec(d),
        pl.BlockSpec((HALO_ROWS, d), lambda i: (jnp.maximum(i * halo_per_tile - 1, 0), 0)),
        pl.BlockSpec((HALO_ROWS, d),
                     lambda i: (jnp.minimum((i + 1) * halo_per_tile, n_halo - 1), 0)),
        _const_spec((1, d)),
        _layer_spec(w_xbc, layer), _layer_spec(w_rest, layer), _layer_spec(w_dtT, layer),
        _const_spec(conv_w.shape), _const_spec((1, CONV_CH)),
    ]
    out_shape = [
        jax.ShapeDtypeStruct((m, REST_WIDTH), BF16),
        jax.ShapeDtypeStruct((bsz, D_SSD, seq), BF16),
        jax.ShapeDtypeStruct((m, 2 * BC_WIDTH), BF16),
        jax.ShapeDtypeStruct((2 * SSD_HEADS, m), F32),
    ]
    out_specs = [
        row_spec(REST_WIDTH),
        pl.BlockSpec((1, D_SSD, rows), lambda i: (i // tiles_per_seq, 0, i % tiles_per_seq)),
        row_spec(2 * BC_WIDTH),
        pl.BlockSpec((2 * SSD_HEADS, rows), lambda i: (0, i)),
    ]
    return pl.pallas_call(
        functools.partial(_in_proj_kernel, tiles_per_seq),
        grid=(n_tiles,),
        in_specs=in_specs,
        out_specs=out_specs,
        out_shape=out_shape,
        scratch_shapes=[pltpu.VMEM((CONV_CH // CONV_STRIP, rows + 2 * HALO_ROWS, CONV_STRIP), F32),
                        pltpu.VMEM((XBC_STEP // CONV_STRIP, rows, CONV_STRIP), F32),
                        pltpu.VMEM((rows + 2 * HALO_ROWS, d), BF16)],
        compiler_params=pltpu.CompilerParams(
            dimension_semantics=("arbitrary",), vmem_limit_bytes=VMEM_LIMIT_BYTES),
        name="in_proj",
    )(x2, x2, x2, norm_g.reshape(1, d), w_xbc, w_rest, w_dtT, conv_w, conv_b.reshape(1, CONV_CH))


def _expand_rows(v):
    lanes = v.shape[1]
    return jnp.concatenate(
        [jnp.broadcast_to(v[j:j + 1, :], (SSD_HEAD_DIM, lanes)) for j in range(v.shape[0])], axis=0)


TAB_QF, TAB_QB, TAB_EF, TAB_EB, TAB_WF, TAB_WB, TAB_DECAY_F, TAB_DECAY_B, TAB_DTB = range(9)
LOG2E = 1.4426950408889634


def _ssd_kernel(n_chunks,
                xsT_ref, b_ref, c_ref, dtT_ref, bias_col_ref, alog_col_ref, dskip_ref,
                y_ref,
                snap_ref, state_ref, tab_ref, colt_ref):
    H = SSD_HEADS
    row_id = lax.broadcasted_iota(jnp.int32, (CHUNK, CHUNK), 0)
    col_id = lax.broadcasted_iota(jnp.int32, (CHUNK, CHUNK), 1)
    lower = row_id >= col_id
    diag = row_id == col_id
    lower_f = lower.astype(F32)
    upper_f = (row_id <= col_id).astype(F32)

    dt_all = _softplus(dtT_ref[...] + bias_col_ref[...])
    a_all = dt_all * -jnp.exp(alog_col_ref[...])

    def stack(v, lo):
        return jnp.concatenate(
            [v[lo:lo + H, c * CHUNK:(c + 1) * CHUNK] for c in range(n_chunks)], axis=0)

    dt_f, dt_b = stack(dt_all, 0), stack(dt_all, H)
    cum_f = _dot_exact(stack(a_all, 0), upper_f)
    cum_b = _dot_exact(stack(a_all, H), lower_f)
    total_f = jnp.broadcast_to(cum_f[:, CHUNK - 1:CHUNK], cum_f.shape)
    total_b = jnp.broadcast_to(cum_b[:, 0:1], cum_b.shape)
    p_f = cum_f * LOG2E
    p_b = cum_b * LOG2E
    tab_ref[TAB_QF] = p_f - jnp.log2(dt_f)
    tab_ref[TAB_QB] = p_b - jnp.log2(dt_b)
    tab_ref[TAB_EF] = jnp.exp(cum_f)
    tab_ref[TAB_EB] = jnp.exp(cum_b)
    tab_ref[TAB_WF] = dt_f * jnp.exp(total_f - cum_f)
    tab_ref[TAB_WB] = dt_b * jnp.exp(total_b - cum_b)
    tab_ref[TAB_DECAY_F] = jnp.exp(total_f)
    tab_ref[TAB_DECAY_B] = jnp.exp(total_b)
    tab_ref[TAB_DTB] = dt_b
    pad = jnp.zeros((CHUNK - 2 * H, CHUNK), F32)
    for c in range(n_chunks):
        rows = slice(c * H, (c + 1) * H)
        colt_ref[c] = jnp.concatenate([p_f[rows], p_b[rows], pad], axis=0).T

    def chunk_rows(k, c):
        return tab_ref[k, pl.ds(pl.multiple_of(c * H, H), H), :]

    def state_update(c, g, xs_t, w, decay):
        hs = slice(g * HEADS_PER_GROUP, (g + 1) * HEADS_PER_GROUP)
        rs = slice(g * GROUP_WIDTH, (g + 1) * GROUP_WIDTH)
        xd_t = (xs_t * _expand_rows(w[hs])).astype(BF16)
        b_g = b_ref[0, pl.ds(c * CHUNK, CHUNK), g * SSD_STATE:(g + 1) * SSD_STATE]
        state_ref[rs, :] = _expand_rows(decay[hs]) * state_ref[rs, :] + _dot(xd_t, b_g)

    state_ref[...] = jnp.zeros_like(state_ref)

    def fwd_body(c, carry):
        snap_ref[c] = state_ref[...].astype(BF16)
        w, decay = chunk_rows(TAB_WF, c), chunk_rows(TAB_DECAY_F, c)
        for g in range(SSD_GROUPS):
            rs = slice(g * GROUP_WIDTH, (g + 1) * GROUP_WIDTH)
            xs_t = xsT_ref[0, rs, pl.ds(c * CHUNK, CHUNK)].astype(F32)
            state_update(c, g, xs_t, w, decay)
        return carry

    lax.fori_loop(0, n_chunks, fwd_body, 0, unroll=math.gcd(n_chunks, FWD_UNROLL))

    state_ref[...] = jnp.zeros_like(state_ref)

    def bwd_body(it, carry):
        c = n_chunks - 1 - it
        q_f, q_b = chunk_rows(TAB_QF, c), chunk_rows(TAB_QB, c)
        e_f, e_b = chunk_rows(TAB_EF, c), chunk_rows(TAB_EB, c)
        w_b, decay_b = chunk_rows(TAB_WB, c), chunk_rows(TAB_DECAY_B, c)
        dt_b_c = chunk_rows(TAB_DTB, c)
        groups = []
        for g in range(SSD_GROUPS):
            hs = slice(g * HEADS_PER_GROUP, (g + 1) * HEADS_PER_GROUP)
            rs = slice(g * GROUP_WIDTH, (g + 1) * GROUP_WIDTH)
            b_g = b_ref[0, pl.ds(c * CHUNK, CHUNK), g * SSD_STATE:(g + 1) * SSD_STATE]
            c_g = c_ref[0, pl.ds(c * CHUNK, CHUNK), g * SSD_STATE:(g + 1) * SSD_STATE]
            xs_bf = xsT_ref[0, rs, pl.ds(c * CHUNK, CHUNK)]
            xs_t = xs_bf.astype(F32)
            cb = _dot_nt(c_g, b_g)
            states = jnp.concatenate([snap_ref[c, rs, :], state_ref[rs, :].astype(BF16)], axis=0)
            y_off = _dot_nt(states, c_g)
            state_update(c, g, xs_t, w_b, decay_b)
            groups.append((hs, rs, xs_bf, xs_t, cb, y_off))
        for g, (hs, rs, xs_bf, xs_t, cb, y_off) in enumerate(groups):
            y_parts = []
            for j in range(HEADS_PER_GROUP):
                hd = g * HEADS_PER_GROUP + j
                arg = jnp.where(lower,
                                colt_ref[c, :, hd:hd + 1] - q_f[hd:hd + 1, :],
                                colt_ref[c, :, H + hd:H + hd + 1] - q_b[hd:hd + 1, :])
                m_h = (cb * jnp.exp2(arg)).astype(BF16)
                y_parts.append(_dot_nt(xs_bf[j * SSD_HEAD_DIM:(j + 1) * SSD_HEAD_DIM, :], m_h))
            cb_diag = jnp.sum(jnp.where(diag, cb, 0.0), axis=0, keepdims=True)
            own = dskip_ref[rs, :] + _expand_rows(cb_diag * dt_b_c[hs])
            y_t = (jnp.concatenate(y_parts, axis=0)
                   + y_off[:GROUP_WIDTH] * _expand_rows(e_f[hs])
                   + y_off[GROUP_WIDTH:] * _expand_rows(e_b[hs])
                   + own * xs_t)
            y_ref[0, pl.ds(c * CHUNK, CHUNK), rs] = y_t.T.astype(BF16)
        return carry

    lax.fori_loop(0, n_chunks, bwd_body, 0, unroll=math.gcd(n_chunks, BWD_UNROLL))


def _ssd(xsT, bc, dtT, dt_bias, a_log, d_skip):
    bsz, _, seq = xsT.shape
    n_chunks = seq // CHUNK
    two_h = 2 * SSD_HEADS
    bias_col = jnp.broadcast_to(dt_bias.reshape(two_h, 1).astype(F32), (two_h, seq))
    alog_col = jnp.broadcast_to(a_log.reshape(two_h, 1).astype(F32), (two_h, seq))
    dskip_col = jnp.broadcast_to(
        jnp.repeat(d_skip.astype(F32), SSD_HEAD_DIM)[:, None], (D_SSD, CHUNK))
    bc3 = bc.reshape(bsz, seq, 2 * BC_WIDTH)
    return pl.pallas_call(
        functools.partial(_ssd_kernel, n_chunks),
        grid=(bsz,),
        in_specs=[
            pl.BlockSpec((1, D_SSD, seq), lambda b: (b, 0, 0)),
            pl.BlockSpec((1, seq, BC_WIDTH), lambda b: (b, 0, 0)),
            pl.BlockSpec((1, seq, BC_WIDTH), lambda b: (b, 0, 1)),
            pl.BlockSpec((two_h, seq), lambda b: (0, b)),
            _const_spec((two_h, seq)), _const_spec((two_h, seq)),
            _const_spec((D_SSD, CHUNK)),
        ],
        out_specs=pl.BlockSpec((1, seq, D_SSD), lambda b: (b, 0, 0)),
        out_shape=jax.ShapeDtypeStruct((bsz, seq, D_SSD), BF16),
        scratch_shapes=[pltpu.VMEM((n_chunks, D_SSD, SSD_STATE), BF16),
                        pltpu.VMEM((D_SSD, SSD_STATE), F32),
                        pltpu.VMEM((TAB_DTB + 1, n_chunks * SSD_HEADS, CHUNK), F32),
                        pltpu.VMEM((n_chunks, CHUNK, CHUNK), F32)],
        compiler_params=pltpu.CompilerParams(
            dimension_semantics=("arbitrary",), vmem_limit_bytes=VMEM_LIMIT_BYTES),
        name="ssd",
    )(xsT, bc3, bc3, dtT, bias_col, alog_col, dskip_col)


def _mix_kernel(apply_final_norm,
                x_ref, ys_ref, rest_ref, ssdg_ref, uf_ref,
                mem_ref, memg_ref, wkv_ref, chan_dft_ref, seq_dft_ref,
                pssd_ref, pfno_ref, pmem_ref, wout_ref, finalg_ref,
                out_ref,
                ab_ref, kv_ref, dft_ref, mid_ref):
    seq = uf_ref.shape[1]
    lt = pl.program_id(1)
    zf_lo = D_SSD + D_FNO
    q_lo = zf_lo + D_FNO
    zm_lo = q_lo + D_MEM
    gates_lo = zm_lo + D_MEM

    @pl.when(lt == 0)
    def _per_sequence():
        half = seq // 2
        r_id = lax.broadcasted_iota(jnp.int32, (CHUNK, 2 * CHUNK), 0)
        c_id = lax.broadcasted_iota(jnp.int32, (CHUNK, 2 * CHUNK), 1)
        reverse = (r_id + c_id == CHUNK).astype(BF16)
        sign = jnp.concatenate([jnp.ones((1, D_FNO), F32), -jnp.ones((1, D_FNO), F32)], axis=1)
        for gi in range(FNO_GROUPS):
            cs = slice(gi * FNO_GROUP_DIM, (gi + 1) * FNO_GROUP_DIM)
            t = _dot(uf_ref[0, :, cs], chan_dft_ref[...]).astype(BF16)
            dft_ref[:, cs] = t[:, :FNO_GROUP_DIM]
            dft_ref[:, D_FNO + gi * FNO_GROUP_DIM:D_FNO + (gi + 1) * FNO_GROUP_DIM] = (
                t[:, FNO_GROUP_DIM:])
        mid_ref[0:1, :] = dft_ref[half:half + 1, :D_FNO].astype(F32)
        for blk in range(half // CHUNK):
            lo = seq - (blk + 1) * CHUNK
            if blk == 0:
                window = jnp.concatenate(
                    [dft_ref[lo:, :], jnp.zeros((CHUNK, 2 * D_FNO), BF16)], axis=0)
            else:
                window = dft_ref[lo:lo + 2 * CHUNK, :]
            own = dft_ref[blk * CHUNK:(blk + 1) * CHUNK, :].astype(F32)
            folded = (own + sign * _dot(reverse, window)).astype(BF16)
            ab_ref[blk * CHUNK:(blk + 1) * CHUNK, :] = folded[:, :D_FNO]
            ab_ref[half + blk * CHUNK:half + (blk + 1) * CHUNK, :] = folded[:, D_FNO:]
        hm = (_rms_scale(mem_ref[0]) * memg_ref[...]).astype(BF16)
        kv_ref[...] = _dot(hm, wkv_ref[...]).astype(BF16)

    yf = _dot(seq_dft_ref[...], ab_ref[...])
    q = rest_ref[0, :, q_lo:zm_lo]
    scores = []
    for hd in range(MEM_HEADS):
        cs = slice(hd * MEM_HEAD_DIM, (hd + 1) * MEM_HEAD_DIM)
        scores.append(_dot_nt(q[:, cs], kv_ref[:, cs]))

    t = ys_ref[0].astype(F32) * _silu(rest_ref[0, :, :D_SSD].astype(F32))
    parts = []
    for gi in range(SSD_GROUPS):
        parts.append(_rms_scale(t[:, gi * GROUP_WIDTH:(gi + 1) * GROUP_WIDTH]))
    ysn = (jnp.concatenate(parts, axis=1) * ssdg_ref[...]).astype(BF16)
    m_s = _dot(ysn, pssd_ref[...])

    k_id = lax.broadcasted_iota(jnp.int32, (x_ref.shape[1], D_FNO), 0)
    alt = jnp.where(k_id % 2 == 0, seq ** -0.5, -(seq ** -0.5))
    yf = (yf + alt * mid_ref[0:1, :]) * _silu(rest_ref[0, :, zf_lo:q_lo].astype(F32))
    m_f = _dot(yf.astype(BF16), pfno_ref[...])

    heads = []
    for hd in range(MEM_HEADS):
        s = scores[hd] * (MEM_HEAD_DIM ** -0.5)
        p = jnp.exp(s - jnp.max(s, axis=-1, keepdims=True))
        denom = jnp.sum(p, axis=-1, keepdims=True)
        pv = _dot(p.astype(BF16), kv_ref[:, D_MEM + hd * MEM_HEAD_DIM:D_MEM + (hd + 1) * MEM_HEAD_DIM])
        heads.append(pv / denom)
    ym = jnp.concatenate(heads, axis=1) * _silu(rest_ref[0, :, zm_lo:gates_lo].astype(F32))
    m_m = _dot(ym.astype(BF16), pmem_ref[...])

    gate = _sigmoid(rest_ref[0, :, gates_lo:].astype(F32))
    merged = (gate[:, :D_MODEL] * m_s + gate[:, D_MODEL:2 * D_MODEL] * m_f
              + gate[:, 2 * D_MODEL:] * m_m)
    out = x_ref[0] + _dot(merged.astype(BF16), wout_ref[...])
    if apply_final_norm:
        out = _rms_scale(out) * finalg_ref[...]
    out_ref[0] = out


def _mix(x, ys, rest, ssd_norm_g, mem, mem_norm_g, chan_dft, seq_dft, w_kv, p_ssd, p_fno, p_mem,
         w_out, layer, final_g, apply_final_norm):
    bsz, seq, d = x.shape
    rows = MIX_ROWS
    n_tiles = seq // rows
    rest3 = rest.reshape(bsz, seq, REST_WIDTH)

    def tile_spec(width):
        return pl.BlockSpec((1, rows, width), lambda b, t: (b, t, 0))

    assert D_SSD == 2 * D_FNO
    in_specs = [
        tile_spec(d), tile_spec(D_SSD), tile_spec(REST_WIDTH), _const_spec((1, D_SSD)),
        pl.BlockSpec((1, seq, D_FNO), lambda b, t: (b, 0, 2)),
        pl.BlockSpec((1, N_MEM, d), lambda b, t: (b, 0, 0)),
        _const_spec((1, d)), _layer_spec(w_kv, layer), _const_spec(chan_dft.shape),
        pl.BlockSpec((rows, seq), lambda b, t: (t, 0)),
        _layer_spec(p_ssd, layer), _layer_spec(p_fno, layer), _layer_spec(p_mem, layer),
        _layer_spec(w_out, layer), _const_spec((1, d)),
    ]
    return pl.pallas_call(
        functools.partial(_mix_kernel, apply_final_norm),
        grid=(bsz, n_tiles),
        in_specs=in_specs,
        out_specs=tile_spec(d),
        out_shape=jax.ShapeDtypeStruct((bsz, seq, d), F32),
        scratch_shapes=[pltpu.VMEM((seq, D_FNO), BF16),
                        pltpu.VMEM((N_MEM, 2 * D_MEM), BF16),
                        pltpu.VMEM((seq, 2 * D_FNO), BF16),
                        pltpu.VMEM((SUBLANES, D_FNO), F32)],
        compiler_params=pltpu.CompilerParams(
            dimension_semantics=("arbitrary", "arbitrary"), vmem_limit_bytes=VMEM_LIMIT_BYTES),
        name="mix",
    )(x, ys, rest3, ssd_norm_g.reshape(1, D_SSD), rest3, mem,
      mem_norm_g.reshape(1, d), w_kv, chan_dft, seq_dft, p_ssd, p_fno, p_mem, w_out,
      final_g.reshape(1, d))


def _dft_table(n, split, sin_sign, n_cols):
    m = jnp.arange(n_cols, dtype=jnp.int32)
    k_hi = jnp.arange(n // split, dtype=jnp.int32) * split
    k_lo = jnp.arange(split, dtype=jnp.int32)
    ang_hi = (2.0 * math.pi / n) * ((k_hi[:, None] * m[None, :]) % n).astype(F32)
    ang_lo = (2.0 * math.pi / n) * ((k_lo[:, None] * m[None, :]) % n).astype(F32)
    ca, sa = jnp.cos(ang_hi), jnp.sin(ang_hi)
    cb, sb = jnp.cos(ang_lo), jnp.sin(ang_lo)
    u = jnp.concatenate([ca, sin_sign * sa], axis=1)[:, None, :]
    v = jnp.concatenate([-sa, sin_sign * ca], axis=1)[:, None, :]
    cb2 = jnp.concatenate([cb, cb], axis=1)[None, :, :]
    sb2 = jnp.concatenate([sb, sb], axis=1)[None, :, :]
    return ((u * cb2 + v * sb2) * (n ** -0.5)).astype(BF16).reshape(n, 2 * n_cols)


def _dft_tables(seq):
    return (_dft_table(FNO_GROUP_DIM, 8, 1.0, FNO_GROUP_DIM),
            _dft_table(seq, 64, -1.0, seq // 2))


def _transpose_cast_kernel(tail_rows, a_ref, b_ref, o_ref):
    a = a_ref[0]
    shifted = jnp.concatenate([a[tail_rows:], b_ref[0]], axis=0)
    rows = jnp.where(pl.program_id(1) == 0, a, shifted) if tail_rows else a
    o_ref[0] = rows.T.astype(BF16)


def _split_w_in(w_in):
    depth, d, _ = w_in.shape
    w_t = jnp.swapaxes(w_in, 1, 2)
    step = W_SPLIT_COLS
    tail = IN_SIZES[2]
    xbc_lo, dt_lo = IN_SIZES[0], IN_SIZES[0] + IN_SIZES[1]
    assert xbc_lo == step and dt_lo % step == 0 and step % tail == 0

    def call(n_steps, a_map, b_map, tail_rows, name):
        return pl.pallas_call(
            functools.partial(_transpose_cast_kernel, tail_rows),
            grid=(depth, n_steps),
            in_specs=[pl.BlockSpec((1, step, d), a_map), pl.BlockSpec((1, tail, d), b_map)],
            out_specs=pl.BlockSpec((1, d, step), lambda i, c: (i, 0, c)),
            out_shape=jax.ShapeDtypeStruct((depth, d, n_steps * step), BF16),
            compiler_params=pltpu.CompilerParams(
                dimension_semantics=("arbitrary", "arbitrary"),
                vmem_limit_bytes=VMEM_LIMIT_BYTES),
            name=name,
        )(w_t, w_t)

    first_rest = dt_lo // step
    w_xbc = call(CONV_CH // step, lambda i, c: (i, xbc_lo // step + c, 0),
                 lambda i, c: (i, 0, 0), 0, "split_w_xbc")
    w_rest = call(REST_WIDTH // step,
                  lambda i, c: (i, jnp.where(c == 0, 0, first_rest - 1 + c), 0),
                  lambda i, c: (i, (first_rest + c) * (step // tail), 0), tail, "split_w_rest")
    w_dt_t = w_t[:, dt_lo:dt_lo + tail, :].astype(BF16)
    return w_xbc, w_rest, w_dt_t


def _layer(x, mem, layer, norm_g, w_in_parts, conv_w, conv_b, dt_bias, a_log, d_skip, ssd_norm_g,
           mem_norm_g, mix_weights, final_g, tables, apply_final_norm):
    w_xbc, w_rest, w_dtT = w_in_parts
    rest, xsT, bc, dtT = _in_proj(x, norm_g, w_xbc, w_rest, w_dtT, layer, conv_w, conv_b)
    ys = _ssd(xsT, bc, dtT, dt_bias, a_log, d_skip)
    return _mix(x, ys, rest, ssd_norm_g, mem, mem_norm_g, *tables, *mix_weights, layer, final_g,
                apply_final_norm)


def kernel(x, mem, norm_g, w_in, conv_w, conv_b, dt_bias, a_log, d_skip, ssd_norm_g, mem_norm_g,
           w_kv, p_ssd, p_fno, p_mem, w_out, final_g):
    depth = norm_g.shape[0]
    tables = _dft_tables(x.shape[1])
    w_in_parts = _split_w_in(w_in)
    mix_weights = tuple(w.astype(BF16) for w in (w_kv, p_ssd, p_fno, p_mem, w_out))
    for i in range(depth):
        x = _layer(x, mem, i, norm_g[i], w_in_parts, conv_w[i], conv_b[i], dt_bias[i], a_log[i],
                   d_skip[i], ssd_norm_g[i], mem_norm_g[i], mix_weights, final_g, tables,
                   i == depth - 1)
    return x
```

```python
import functools
import math

import jax
import jax.numpy as jnp
from jax import lax
from jax.experimental import pallas as pl
from jax.experimental.pallas import tpu as pltpu

F32 = jnp.float32
BF16 = jnp.bfloat16
HIGHEST = lax.Precision.HIGHEST

D_MODEL = 1024
D_SSD = 1024
SSD_HEAD_DIM = 64
SSD_HEADS = D_SSD // SSD_HEAD_DIM
SSD_GROUPS = 4
HEADS_PER_GROUP = SSD_HEADS // SSD_GROUPS
GROUP_WIDTH = HEADS_PER_GROUP * SSD_HEAD_DIM
SSD_STATE = 128
BC_WIDTH = SSD_GROUPS * SSD_STATE
D_CONV = 5
CONV_CH = D_SSD + 2 * BC_WIDTH
CHUNK = 128
D_FNO = 512
FNO_GROUPS = 4
FNO_GROUP_DIM = D_FNO // FNO_GROUPS
N_MEM = 256
MEM_HEADS = 4
D_MEM = 512
MEM_HEAD_DIM = D_MEM // MEM_HEADS
N_BRANCH = 3
EPS = 1e-6
IN_SIZES = (D_SSD, CONV_CH, 2 * SSD_HEADS, D_FNO, D_FNO, D_MEM, D_MEM, N_BRANCH * D_MODEL)
REST_WIDTH = sum(IN_SIZES) - CONV_CH - 2 * SSD_HEADS
XBC_STEP = 256
REST_STEP = REST_WIDTH // (CONV_CH // XBC_STEP)

SUBLANES = 8
HALO_ROWS = 16
CONV_STRIDE = 4
CONV_STRIP = 128
CONV_HALO = D_CONV // 2
IN_PROJ_ROWS = 512
MIX_ROWS = 512
W_SPLIT_COLS = 1024
FWD_UNROLL = 16
BWD_UNROLL = 16
VMEM_LIMIT_BYTES = 56 * 1024 * 1024


def _sigmoid(v):
    return 0.5 * jnp.tanh(0.5 * v) + 0.5


def _silu(v):
    h = 0.5 * v
    return h * (jnp.tanh(h) + 1.0)


def _softplus(v):
    return jnp.maximum(v, 0.0) + jnp.log1p(jnp.exp(-jnp.abs(v)))


def _rms_scale(v):
    return v * lax.rsqrt(jnp.mean(v * v, axis=-1, keepdims=True) + EPS)


def _dot(a, b):
    return jnp.dot(a, b, preferred_element_type=F32)


def _dot_nt(a, b):
    return lax.dot_general(a, b, (((1,), (1,)), ((), ())), preferred_element_type=F32)


def _dot_exact(a, b):
    return jnp.dot(a, b, precision=HIGHEST, preferred_element_type=F32)


def _in_proj_kernel(tiles_per_seq,
                    x_ref, xprev_ref, xnext_ref, g_ref,
                    wxbc_ref, wrest_ref, wdtT_ref, convw_ref, convb_ref,
                    rest_ref, xsT_ref, bc_ref, dtT_ref,
                    stage_ref, act_ref, h_ref):
    rows = x_ref.shape[0]
    lt = pl.program_id(0) % tiles_per_seq
    g = g_ref[...]
    keep_prev = (lt > 0).astype(F32)
    keep_next = (lt < tiles_per_seq - 1).astype(F32)
    h_ref[0:HALO_ROWS, :] = (_rms_scale(xprev_ref[...]) * g * keep_prev).astype(BF16)
    h_ref[HALO_ROWS:HALO_ROWS + rows, :] = (_rms_scale(x_ref[...]) * g).astype(BF16)
    h_ref[HALO_ROWS + rows:, :] = (_rms_scale(xnext_ref[...]) * g * keep_next).astype(BF16)
    h_own = h_ref.at[HALO_ROWS:HALO_ROWS + rows, :]

    strips_per_step = XBC_STEP // CONV_STRIP

    def project_xbc(j):
        res = _dot(h_ref[...], wxbc_ref[:, pl.ds(j * XBC_STEP, XBC_STEP)])
        for t in range(strips_per_step):
            stage_ref[j * strips_per_step + t] = res[:, t * CONV_STRIP:(t + 1) * CONV_STRIP]

    def project_rest(j):
        cols = pl.ds(j * REST_STEP, REST_STEP)
        rest_ref[:, cols] = _dot(h_own[...], wrest_ref[:, cols]).astype(BF16)

    def conv(j, is_x):
        block = SUBLANES * CONV_STRIDE
        for t in range(strips_per_step):
            slab = j * strips_per_step + t
            cs = pl.ds(slab * CONV_STRIP, CONV_STRIP)
            bias = convb_ref[:, cs]
            taps = [convw_ref[k:k + 1, cs] for k in range(D_CONV)]
            for r0 in range(0, rows, block):
                base = HALO_ROWS + r0 - CONV_HALO
                regs = [stage_ref[slab, pl.ds(base + m, SUBLANES, stride=CONV_STRIDE), :]
                        for m in range(CONV_STRIDE + 2 * CONV_HALO)]
                for i in range(CONV_STRIDE):
                    acc = bias + taps[0] * regs[i]
                    for k in range(1, D_CONV):
                        acc = acc + taps[k] * regs[i + k]
                    act_ref[t, pl.ds(r0 + i, SUBLANES, stride=CONV_STRIDE), :] = _silu(acc)
            if is_x:
                for r0 in range(0, rows, CHUNK):
                    xsT_ref[0, cs, r0:r0 + CHUNK] = act_ref[t, r0:r0 + CHUNK, :].T.astype(BF16)
            else:
                bc_ref[:, pl.ds(slab * CONV_STRIP - D_SSD, CONV_STRIP)] = act_ref[t].astype(BF16)

    project_xbc(0)
    dtT_ref[...] = _dot_nt(wdtT_ref[...], h_own[...])

    n_steps = CONV_CH // XBC_STEP
    for j in range(n_steps):
        conv(j, j * XBC_STEP < D_SSD)
        if j + 1 < n_steps:
            project_xbc(j + 1)
        project_rest(j)


def _const_spec(shape):
    nd = len(shape)
    return pl.BlockSpec(shape, lambda *_: (0,) * nd, pipeline_mode=pl.Buffered(1))


def _layer_spec(stacked, layer):
    return pl.BlockSpec((None,) + stacked.shape[1:], lambda *_: (layer, 0, 0),
                        pipeline_mode=pl.Buffered(1))


def _in_proj(x, norm_g, w_xbc, w_rest, w_dtT, layer, conv_w, conv_b):
    bsz, seq, d = x.shape
    m = bsz * seq
    rows = IN_PROJ_ROWS
    tiles_per_seq = seq // rows
    n_tiles = m // rows
    halo_per_tile = rows // HALO_ROWS
    n_halo = m // HALO_ROWS
    x2 = x.reshape(m, d)

    def row_spec(width):
        return pl.BlockSpec((rows, width), lambda i: (i, 0))

    in_specs = [
        row_spec(d),
        pl.BlockSpec((HALO_ROWS, d), lambda i: (jnp.maximum(i * halo_per_tile - 1, 0), 0)),
        pl.BlockSpec((HALO_ROWS, d),
                     lambda i: (jnp.minimum((i + 1) * halo_per_tile, n_halo - 1), 0)),
        _const_spec((1, d)),
        _layer_spec(w_xbc, layer), _layer_spec(w_rest, layer), _layer_spec(w_dtT, layer),
        _const_spec(conv_w.shape), _const_spec((1, CONV_CH)),
    ]
    out_shape = [
        jax.ShapeDtypeStruct((m, REST_WIDTH), BF16),
        jax.ShapeDtypeStruct((bsz, D_SSD, seq), BF16),
        jax.ShapeDtypeStruct((m, 2 * BC_WIDTH), BF16),
        jax.ShapeDtypeStruct((2 * SSD_HEADS, m), F32),
    ]
    out_specs = [
        row_spec(REST_WIDTH),
        pl.BlockSpec((1, D_SSD, rows), lambda i: (i // tiles_per_seq, 0, i % tiles_per_seq)),
        row_spec(2 * BC_WIDTH),
        pl.BlockSpec((2 * SSD_HEADS, rows), lambda i: (0, i)),
    ]
    return pl.pallas_call(
        functools.partial(_in_proj_kernel, tiles_per_seq),
        grid=(n_tiles,),
        in_specs=in_specs,
        out_specs=out_specs,
        out_shape=out_shape,
        scratch_shapes=[pltpu.VMEM((CONV_CH // CONV_STRIP, rows + 2 * HALO_ROWS, CONV_STRIP), F32),
                        pltpu.VMEM((XBC_STEP // CONV_STRIP, rows, CONV_STRIP), F32),
                        pltpu.VMEM((rows + 2 * HALO_ROWS, d), BF16)],
        compiler_params=pltpu.CompilerParams(
            dimension_semantics=("arbitrary",), vmem_limit_bytes=VMEM_LIMIT_BYTES),
        name="in_proj",
    )(x2, x2, x2, norm_g.reshape(1, d), w_xbc, w_rest, w_dtT, conv_w, conv_b.reshape(1, CONV_CH))


def _expand_rows(v):
    lanes = v.shape[1]
    return jnp.concatenate(
        [jnp.broadcast_to(v[j:j + 1, :], (SSD_HEAD_DIM, lanes)) for j in range(v.shape[0])], axis=0)


(TAB_QF, TAB_QB, TAB_EF, TAB_EB, TAB_WF, TAB_WB, TAB_DECAY_F, TAB_DECAY_B, TAB_DTB,
 TAB_PF, TAB_PB) = range(11)
LOG2E = 1.4426950408889634


def _ssd_kernel(n_chunks,
                xsT_ref, b_ref, c_ref, dtT_ref, bias_col_ref, alog_col_ref, dskip_ref,
                y_ref,
                snap_ref, state_ref, tab_ref):
    H = SSD_HEADS
    row_id = lax.broadcasted_iota(jnp.int32, (CHUNK, CHUNK), 0)
    col_id = lax.broadcasted_iota(jnp.int32, (CHUNK, CHUNK), 1)
    lower = row_id >= col_id
    diag = row_id == col_id
    lower_f = lower.astype(F32)
    upper_f = (row_id <= col_id).astype(F32)

    dt_all = _softplus(dtT_ref[...] + bias_col_ref[...])
    a_all = dt_all * -jnp.exp(alog_col_ref[...])

    def stack(v, lo):
        return jnp.concatenate(
            [v[lo:lo + H, c * CHUNK:(c + 1) * CHUNK] for c in range(n_chunks)], axis=0)

    dt_f, dt_b = stack(dt_all, 0), stack(dt_all, H)
    cum_f = _dot_exact(stack(a_all, 0), upper_f)
    cum_b = _dot_exact(stack(a_all, H), lower_f)
    total_f = jnp.broadcast_to(cum_f[:, CHUNK - 1:CHUNK], cum_f.shape)
    total_b = jnp.broadcast_to(cum_b[:, 0:1], cum_b.shape)
    p_f = cum_f * LOG2E
    p_b = cum_b * LOG2E
    tab_ref[TAB_QF] = p_f - jnp.log2(dt_f)
    tab_ref[TAB_QB] = p_b - jnp.log2(dt_b)
    tab_ref[TAB_EF] = jnp.exp(cum_f)
    tab_ref[TAB_EB] = jnp.exp(cum_b)
    tab_ref[TAB_WF] = dt_f * jnp.exp(total_f - cum_f)
    tab_ref[TAB_WB] = dt_b * jnp.exp(total_b - cum_b)
    tab_ref[TAB_DECAY_F] = jnp.exp(total_f)
    tab_ref[TAB_DECAY_B] = jnp.exp(total_b)
    tab_ref[TAB_DTB] = dt_b
    tab_ref[TAB_PF] = p_f
    tab_ref[TAB_PB] = p_b
    def chunk_rows(k, c):
        return tab_ref[k, pl.ds(pl.multiple_of(c * H, H), H), :]

    def state_update(c, g, xs_t, w, decay):
        hs = slice(g * HEADS_PER_GROUP, (g + 1) * HEADS_PER_GROUP)
        rs = slice(g * GROUP_WIDTH, (g + 1) * GROUP_WIDTH)
        xd_t = (xs_t * _expand_rows(w[hs])).astype(BF16)
        b_g = b_ref[0, pl.ds(c * CHUNK, CHUNK), g * SSD_STATE:(g + 1) * SSD_STATE]
        state_ref[rs, :] = _expand_rows(decay[hs]) * state_ref[rs, :] + _dot(xd_t, b_g)

    state_ref[...] = jnp.zeros_like(state_ref)

    def fwd_body(c, carry):
        snap_ref[c] = state_ref[...].astype(BF16)
        w, decay = chunk_rows(TAB_WF, c), chunk_rows(TAB_DECAY_F, c)
        for g in range(SSD_GROUPS):
            rs = slice(g * GROUP_WIDTH, (g + 1) * GROUP_WIDTH)
            xs_t = xsT_ref[0, rs, pl.ds(c * CHUNK, CHUNK)].astype(F32)
            state_update(c, g, xs_t, w, decay)
        return carry

    lax.fori_loop(0, n_chunks, fwd_body, 0, unroll=math.gcd(n_chunks, FWD_UNROLL))

    state_ref[...] = jnp.zeros_like(state_ref)

    def bwd_body(it, carry):
        c = n_chunks - 1 - it
        q_f, q_b = chunk_rows(TAB_QF, c), chunk_rows(TAB_QB, c)
        e_f, e_b = chunk_rows(TAB_EF, c), chunk_rows(TAB_EB, c)
        w_b, decay_b = chunk_rows(TAB_WB, c), chunk_rows(TAB_DECAY_B, c)
        dt_b_c = chunk_rows(TAB_DTB, c)
        p_f_c, p_b_c = chunk_rows(TAB_PF, c), chunk_rows(TAB_PB, c)
        groups = []
        for g in range(SSD_GROUPS):
            hs = slice(g * HEADS_PER_GROUP, (g + 1) * HEADS_PER_GROUP)
            rs = slice(g * GROUP_WIDTH, (g + 1) * GROUP_WIDTH)
            b_g = b_ref[0, pl.ds(c * CHUNK, CHUNK), g * SSD_STATE:(g + 1) * SSD_STATE]
            c_g = c_ref[0, pl.ds(c * CHUNK, CHUNK), g * SSD_STATE:(g + 1) * SSD_STATE]
            xs_bf = xsT_ref[0, rs, pl.ds(c * CHUNK, CHUNK)]
            xs_t = xs_bf.astype(F32)
            cb = _dot_nt(c_g, b_g)
            states = jnp.concatenate([snap_ref[c, rs, :], state_ref[rs, :].astype(BF16)], axis=0)
            y_off = _dot_nt(states, c_g)
            state_update(c, g, xs_t, w_b, decay_b)
            groups.append((hs, rs, xs_bf, xs_t, cb, y_off))
        for g, (hs, rs, xs_bf, xs_t, cb, y_off) in enumerate(groups):
            y_parts = []
            for j in range(HEADS_PER_GROUP):
                hd = g * HEADS_PER_GROUP + j
                tgt_f = jnp.broadcast_to(p_f_c[hd:hd + 1, :], (CHUNK, CHUNK)).T
                tgt_b = jnp.broadcast_to(p_b_c[hd:hd + 1, :], (CHUNK, CHUNK)).T
                arg = jnp.where(lower, tgt_f - q_f[hd:hd + 1, :], tgt_b - q_b[hd:hd + 1, :])
                m_h = (cb * jnp.exp2(arg)).astype(BF16)
                y_parts.append(_dot_nt(xs_bf[j * SSD_HEAD_DIM:(j + 1) * SSD_HEAD_DIM, :], m_h))
            cb_diag = jnp.sum(jnp.where(diag, cb, 0.0), axis=0, keepdims=True)
            own = dskip_ref[rs, :] + _expand_rows(cb_diag * dt_b_c[hs])
            y_t = (jnp.concatenate(y_parts, axis=0)
                   + y_off[:GROUP_WIDTH] * _expand_rows(e_f[hs])
                   + y_off[GROUP_WIDTH:] * _expand_rows(e_b[hs])
                   + own * xs_t)
            y_ref[0, pl.ds(c * CHUNK, CHUNK), rs] = y_t.T.astype(BF16)
        return carry

    lax.fori_loop(0, n_chunks, bwd_body, 0, unroll=math.gcd(n_chunks, BWD_UNROLL))


def _ssd(xsT, bc, dtT, dt_bias, a_log, d_skip):
    bsz, _, seq = xsT.shape
    n_chunks = seq // CHUNK
    two_h = 2 * SSD_HEADS
    bias_col = jnp.broadcast_to(dt_bias.reshape(two_h, 1).astype(F32), (two_h, seq))
    alog_col = jnp.broadcast_to(a_log.reshape(two_h, 1).astype(F32), (two_h, seq))
    dskip_col = jnp.broadcast_to(
        jnp.repeat(d_skip.astype(F32), SSD_HEAD_DIM)[:, None], (D_SSD, CHUNK))
    bc3 = bc.reshape(bsz, seq, 2 * BC_WIDTH)
    return pl.pallas_call(
        functools.partial(_ssd_kernel, n_chunks),
        grid=(bsz,),
        in_specs=[
            pl.BlockSpec((1, D_SSD, seq), lambda b: (b, 0, 0)),
            pl.BlockSpec((1, seq, BC_WIDTH), lambda b: (b, 0, 0)),
            pl.BlockSpec((1, seq, BC_WIDTH), lambda b: (b, 0, 1)),
            pl.BlockSpec((two_h, seq), lambda b: (0, b)),
            _const_spec((two_h, seq)), _const_spec((two_h, seq)),
            _const_spec((D_SSD, CHUNK)),
        ],
        out_specs=pl.BlockSpec((1, seq, D_SSD), lambda b: (b, 0, 0)),
        out_shape=jax.ShapeDtypeStruct((bsz, seq, D_SSD), BF16),
        scratch_shapes=[pltpu.VMEM((n_chunks, D_SSD, SSD_STATE), BF16),
                        pltpu.VMEM((D_SSD, SSD_STATE), F32),
                        pltpu.VMEM((TAB_PB + 1, n_chunks * SSD_HEADS, CHUNK), F32)],
        compiler_params=pltpu.CompilerParams(
            dimension_semantics=("arbitrary",), vmem_limit_bytes=VMEM_LIMIT_BYTES),
        name="ssd",
    )(xsT, bc3, bc3, dtT, bias_col, alog_col, dskip_col)


def _mix_kernel(apply_final_norm,
                x_ref, ys_ref, rest_ref, ssdg_ref, uf_ref,
                mem_ref, memg_ref, wkv_ref, chan_dft_ref, seq_dft_ref,
                pssd_ref, pfno_ref, pmem_ref, wout_ref, finalg_ref,
                out_ref,
                ab_ref, kv_ref, dft_ref, mid_ref):
    seq = uf_ref.shape[1]
    lt = pl.program_id(1)
    zf_lo = D_SSD + D_FNO
    q_lo = zf_lo + D_FNO
    zm_lo = q_lo + D_MEM
    gates_lo = zm_lo + D_MEM

    @pl.when(lt == 0)
    def _per_sequence():
        half = seq // 2
        r_id = lax.broadcasted_iota(jnp.int32, (CHUNK, 2 * CHUNK), 0)
        c_id = lax.broadcasted_iota(jnp.int32, (CHUNK, 2 * CHUNK), 1)
        reverse = (r_id + c_id == CHUNK).astype(BF16)
        sign = jnp.concatenate([jnp.ones((1, D_FNO), F32), -jnp.ones((1, D_FNO), F32)], axis=1)
        for gi in range(FNO_GROUPS):
            cs = slice(gi * FNO_GROUP_DIM, (gi + 1) * FNO_GROUP_DIM)
            t = _dot(uf_ref[0, :, cs], chan_dft_ref[...]).astype(BF16)
            dft_ref[:, cs] = t[:, :FNO_GROUP_DIM]
            dft_ref[:, D_FNO + gi * FNO_GROUP_DIM:D_FNO + (gi + 1) * FNO_GROUP_DIM] = (
                t[:, FNO_GROUP_DIM:])
        mid_ref[0:1, :] = dft_ref[half:half + 1, :D_FNO].astype(F32)
        for blk in range(half // CHUNK):
            lo = seq - (blk + 1) * CHUNK
            if blk == 0:
                window = jnp.concatenate(
                    [dft_ref[lo:, :], jnp.zeros((CHUNK, 2 * D_FNO), BF16)], axis=0)
            else:
                window = dft_ref[lo:lo + 2 * CHUNK, :]
            own = dft_ref[blk * CHUNK:(blk + 1) * CHUNK, :].astype(F32)
            folded = (own + sign * _dot(reverse, window)).astype(BF16)
            ab_ref[blk * CHUNK:(blk + 1) * CHUNK, :] = folded[:, :D_FNO]
            ab_ref[half + blk * CHUNK:half + (blk + 1) * CHUNK, :] = folded[:, D_FNO:]
        hm = (_rms_scale(mem_ref[0]) * memg_ref[...]).astype(BF16)
        kv_ref[...] = _dot(hm, wkv_ref[...]).astype(BF16)

    yf = _dot(seq_dft_ref[...], ab_ref[...])
    q = rest_ref[0, :, q_lo:zm_lo]
    scores = []
    for hd in range(MEM_HEADS):
        cs = slice(hd * MEM_HEAD_DIM, (hd + 1) * MEM_HEAD_DIM)
        scores.append(_dot_nt(q[:, cs], kv_ref[:, cs]))

    t = ys_ref[0].astype(F32) * _silu(rest_ref[0, :, :D_SSD].astype(F32))
    parts = []
    for gi in range(SSD_GROUPS):
        parts.append(_rms_scale(t[:, gi * GROUP_WIDTH:(gi + 1) * GROUP_WIDTH]))
    ysn = (jnp.concatenate(parts, axis=1) * ssdg_ref[...]).astype(BF16)
    m_s = _dot(ysn, pssd_ref[...])

    k_id = lax.broadcasted_iota(jnp.int32, (x_ref.shape[1], D_FNO), 0)
    alt = jnp.where(k_id % 2 == 0, seq ** -0.5, -(seq ** -0.5))
    yf = (yf + alt * mid_ref[0:1, :]) * _silu(rest_ref[0, :, zf_lo:q_lo].astype(F32))
    m_f = _dot(yf.astype(BF16), pfno_ref[...])

    heads = []
    for hd in range(MEM_HEADS):
        s = scores[hd] * (MEM_HEAD_DIM ** -0.5)
        p = jnp.exp(s - jnp.max(s, axis=-1, keepdims=True))
        denom = jnp.sum(p, axis=-1, keepdims=True)
        pv = _dot(p.astype(BF16), kv_ref[:, D_MEM + hd * MEM_HEAD_DIM:D_MEM + (hd + 1) * MEM_HEAD_DIM])
        heads.append(pv / denom)
    ym = jnp.concatenate(heads, axis=1) * _silu(rest_ref[0, :, zm_lo:gates_lo].astype(F32))
    m_m = _dot(ym.astype(BF16), pmem_ref[...])

    gate = _sigmoid(rest_ref[0, :, gates_lo:].astype(F32))
    merged = (gate[:, :D_MODEL] * m_s + gate[:, D_MODEL:2 * D_MODEL] * m_f
              + gate[:, 2 * D_MODEL:] * m_m)
    out = x_ref[0] + _dot(merged.astype(BF16), wout_ref[...])
    if apply_final_norm:
        out = _rms_scale(out) * finalg_ref[...]
    out_ref[0] = out


def _mix(x, ys, rest, ssd_norm_g, mem, mem_norm_g, chan_dft, seq_dft, w_kv, p_ssd, p_fno, p_mem,
         w_out, layer, final_g, apply_final_norm):
    bsz, seq, d = x.shape
    rows = MIX_ROWS
    n_tiles = seq // rows
    rest3 = rest.reshape(bsz, seq, REST_WIDTH)

    def tile_spec(width):
        return pl.BlockSpec((1, rows, width), lambda b, t: (b, t, 0))

    assert D_SSD == 2 * D_FNO
    in_specs = [
        tile_spec(d), tile_spec(D_SSD), tile_spec(REST_WIDTH), _const_spec((1, D_SSD)),
        pl.BlockSpec((1, seq, D_FNO), lambda b, t: (b, 0, 2)),
        pl.BlockSpec((1, N_MEM, d), lambda b, t: (b, 0, 0)),
        _const_spec((1, d)), _layer_spec(w_kv, layer), _const_spec(chan_dft.shape),
        pl.BlockSpec((rows, seq), lambda b, t: (t, 0)),
        _layer_spec(p_ssd, layer), _layer_spec(p_fno, layer), _layer_spec(p_mem, layer),
        _layer_spec(w_out, layer), _const_spec((1, d)),
    ]
    return pl.pallas_call(
        functools.partial(_mix_kernel, apply_final_norm),
        grid=(bsz, n_tiles),
        in_specs=in_specs,
        out_specs=tile_spec(d),
        out_shape=jax.ShapeDtypeStruct((bsz, seq, d), F32),
        scratch_shapes=[pltpu.VMEM((seq, D_FNO), BF16),
                        pltpu.VMEM((N_MEM, 2 * D_MEM), BF16),
                        pltpu.VMEM((seq, 2 * D_FNO), BF16),
                        pltpu.VMEM((SUBLANES, D_FNO), F32)],
        compiler_params=pltpu.CompilerParams(
            dimension_semantics=("arbitrary", "arbitrary"), vmem_limit_bytes=VMEM_LIMIT_BYTES),
        name="mix",
    )(x, ys, rest3, ssd_norm_g.reshape(1, D_SSD), rest3, mem,
      mem_norm_g.reshape(1, d), w_kv, chan_dft, seq_dft, p_ssd, p_fno, p_mem, w_out,
      final_g.reshape(1, d))


def _dft_table(n, split, sin_sign, n_cols):
    m = jnp.arange(n_cols, dtype=jnp.int32)
    k_hi = jnp.arange(n // split, dtype=jnp.int32) * split
    k_lo = jnp.arange(split, dtype=jnp.int32)
    ang_hi = (2.0 * math.pi / n) * ((k_hi[:, None] * m[None, :]) % n).astype(F32)
    ang_lo = (2.0 * math.pi / n) * ((k_lo[:, None] * m[None, :]) % n).astype(F32)
    ca, sa = jnp.cos(ang_hi), jnp.sin(ang_hi)
    cb, sb = jnp.cos(ang_lo), jnp.sin(ang_lo)
    u = jnp.concatenate([ca, sin_sign * sa], axis=1)[:, None, :]
    v = jnp.concatenate([-sa, sin_sign * ca], axis=1)[:, None, :]
    cb2 = jnp.concatenate([cb, cb], axis=1)[None, :, :]
    sb2 = jnp.concatenate([sb, sb], axis=1)[None, :, :]
    return ((u * cb2 + v * sb2) * (n ** -0.5)).astype(BF16).reshape(n, 2 * n_cols)


def _dft_tables(seq):
    return (_dft_table(FNO_GROUP_DIM, 8, 1.0, FNO_GROUP_DIM),
            _dft_table(seq, 64, -1.0, seq // 2))


def _transpose_cast_kernel(tail_rows, a_ref, b_ref, o_ref):
    a = a_ref[0]
    shifted = jnp.concatenate([a[tail_rows:], b_ref[0]], axis=0)
    rows = jnp.where(pl.program_id(1) == 0, a, shifted) if tail_rows else a
    o_ref[0] = rows.T.astype(BF16)


def _split_w_in(w_in):
    depth, d, _ = w_in.shape
    w_t = jnp.swapaxes(w_in, 1, 2)
    step = W_SPLIT_COLS
    tail = IN_SIZES[2]
    xbc_lo, dt_lo = IN_SIZES[0], IN_SIZES[0] + IN_SIZES[1]
    assert xbc_lo == step and dt_lo % step == 0 and step % tail == 0

    def call(n_steps, a_map, b_map, tail_rows, name):
        return pl.pallas_call(
            functools.partial(_transpose_cast_kernel, tail_rows),
            grid=(depth, n_steps),
            in_specs=[pl.BlockSpec((1, step, d), a_map), pl.BlockSpec((1, tail, d), b_map)],
            out_specs=pl.BlockSpec((1, d, step), lambda i, c: (i, 0, c)),
            out_shape=jax.ShapeDtypeStruct((depth, d, n_steps * step), BF16),
            compiler_params=pltpu.CompilerParams(
                dimension_semantics=("arbitrary", "arbitrary"),
                vmem_limit_bytes=VMEM_LIMIT_BYTES),
            name=name,
        )(w_t, w_t)

    first_rest = dt_lo // step
    w_xbc = call(CONV_CH // step, lambda i, c: (i, xbc_lo // step + c, 0),
                 lambda i, c: (i, 0, 0), 0, "split_w_xbc")
    w_rest = call(REST_WIDTH // step,
                  lambda i, c: (i, jnp.where(c == 0, 0, first_rest - 1 + c), 0),
                  lambda i, c: (i, (first_rest + c) * (step // tail), 0), tail, "split_w_rest")
    w_dt_t = w_t[:, dt_lo:dt_lo + tail, :].astype(BF16)
    return w_xbc, w_rest, w_dt_t


def _layer(x, mem, layer, norm_g, w_in_parts, conv_w, conv_b, dt_bias, a_log, d_skip, ssd_norm_g,
           mem_norm_g, mix_weights, final_g, tables, apply_final_norm):
    w_xbc, w_rest, w_dtT = w_in_parts
    rest, xsT, bc, dtT = _in_proj(x, norm_g, w_xbc, w_rest, w_dtT, layer, conv_w, conv_b)
    ys = _ssd(xsT, bc, dtT, dt_bias, a_log, d_skip)
    return _mix(x, ys, rest, ssd_norm_g, mem, mem_norm_g, *tables, *mix_weights, layer, final_g,
                apply_final_norm)


def kernel(x, mem, norm_g, w_in, conv_w, conv_b, dt_bias, a_log, d_skip, ssd_norm_g, mem_norm_g,
           w_kv, p_ssd, p_fno, p_mem, w_out, final_g):
    depth = norm_g.shape[0]
    tables = _dft_tables(x.shape[1])
    w_in_parts = _split_w_in(w_in)
    mix_weights = tuple(w.astype(BF16) for w in (w_kv, p_ssd, p_fno, p_mem, w_out))
    for i in range(depth):
        x = _layer(x, mem, i, norm_g[i], w_in_parts, conv_w[i], conv_b[i], dt_bias[i], a_log[i],
                   d_skip[i], ssd_norm_g[i], mem_norm_g[i], mix_weights, final_g, tables,
                   i == depth - 1)
    return x
```

```python
import functools
import math

import jax
import jax.numpy as jnp
from jax import lax
from jax.experimental import pallas as pl
from jax.experimental.pallas import tpu as pltpu

F32 = jnp.float32
BF16 = jnp.bfloat16
HIGHEST = lax.Precision.HIGHEST

D_MODEL = 1024
D_SSD = 1024
SSD_HEAD_DIM = 64
SSD_HEADS = D_SSD // SSD_HEAD_DIM
SSD_GROUPS = 4
HEADS_PER_GROUP = SSD_HEADS // SSD_GROUPS
GROUP_WIDTH = HEADS_PER_GROUP * SSD_HEAD_DIM
SSD_STATE = 128
BC_WIDTH = SSD_GROUPS * SSD_STATE
D_CONV = 5
CONV_CH = D_SSD + 2 * BC_WIDTH
CHUNK = 128
D_FNO = 512
FNO_GROUPS = 4
FNO_GROUP_DIM = D_FNO // FNO_GROUPS
N_MEM = 256
MEM_HEADS = 4
D_MEM = 512
MEM_HEAD_DIM = D_MEM // MEM_HEADS
N_BRANCH = 3
EPS = 1e-6
IN_SIZES = (D_SSD, CONV_CH, 2 * SSD_HEADS, D_FNO, D_FNO, D_MEM, D_MEM, N_BRANCH * D_MODEL)
REST_WIDTH = sum(IN_SIZES) - CONV_CH - 2 * SSD_HEADS
XBC_STEP = 256
REST_STEP = REST_WIDTH // (CONV_CH // XBC_STEP)

SUBLANES = 8
HALO_ROWS = 16
CONV_STRIDE = 4
CONV_STRIP = 128
CONV_HALO = D_CONV // 2
IN_PROJ_ROWS = 512
MIX_ROWS = 512
W_SPLIT_COLS = 1024
FWD_UNROLL = 16
BWD_UNROLL = 16
VMEM_LIMIT_BYTES = 56 * 1024 * 1024


def _sigmoid(v):
    return 0.5 * jnp.tanh(0.5 * v) + 0.5


def _silu(v):
    h = 0.5 * v
    return h * (jnp.tanh(h) + 1.0)


def _softplus(v):
    return jnp.maximum(v, 0.0) + jnp.log1p(jnp.exp(-jnp.abs(v)))


def _rms_scale(v):
    return v * lax.rsqrt(jnp.mean(v * v, axis=-1, keepdims=True) + EPS)


def _dot(a, b):
    return jnp.dot(a, b, preferred_element_type=F32)


def _dot_nt(a, b):
    return lax.dot_general(a, b, (((1,), (1,)), ((), ())), preferred_element_type=F32)


def _dot_exact(a, b):
    return jnp.dot(a, b, precision=HIGHEST, preferred_element_type=F32)


def _in_proj_kernel(tiles_per_seq,
                    x_ref, xprev_ref, xnext_ref, g_ref,
                    wxbc_ref, wrest_ref, wdtT_ref, convw_ref, convb_ref,
                    rest_ref, xsT_ref, bc_ref, dtT_ref,
                    stage_ref, act_ref, h_ref):
    rows = x_ref.shape[0]
    lt = pl.program_id(0) % tiles_per_seq
    g = g_ref[...]
    keep_prev = (lt > 0).astype(F32)
    keep_next = (lt < tiles_per_seq - 1).astype(F32)
    h_ref[0:HALO_ROWS, :] = (_rms_scale(xprev_ref[...]) * g * keep_prev).astype(BF16)
    h_ref[HALO_ROWS:HALO_ROWS + rows, :] = (_rms_scale(x_ref[...]) * g).astype(BF16)
    h_ref[HALO_ROWS + rows:, :] = (_rms_scale(xnext_ref[...]) * g * keep_next).astype(BF16)
    h_own = h_ref.at[HALO_ROWS:HALO_ROWS + rows, :]

    strips_per_step = XBC_STEP // CONV_STRIP

    def project_xbc(j):
        res = _dot(h_ref[...], wxbc_ref[:, pl.ds(j * XBC_STEP, XBC_STEP)])
        for t in range(strips_per_step):
            stage_ref[j * strips_per_step + t] = res[:, t * CONV_STRIP:(t + 1) * CONV_STRIP]

    def project_rest(j):
        cols = pl.ds(j * REST_STEP, REST_STEP)
        rest_ref[:, cols] = _dot(h_own[...], wrest_ref[:, cols]).astype(BF16)

    def conv(j, is_x):
        block = SUBLANES * CONV_STRIDE
        for t in range(strips_per_step):
            slab = j * strips_per_step + t
            cs = pl.ds(slab * CONV_STRIP, CONV_STRIP)
            bias = convb_ref[:, cs]
            taps = [convw_ref[k:k + 1, cs] for k in range(D_CONV)]
            for r0 in range(0, rows, block):
                base = HALO_ROWS + r0 - CONV_HALO
                regs = [stage_ref[slab, pl.ds(base + m, SUBLANES, stride=CONV_STRIDE), :]
                        for m in range(CONV_STRIDE + 2 * CONV_HALO)]
                for i in range(CONV_STRIDE):
                    acc = bias + taps[0] * regs[i]
                    for k in range(1, D_CONV):
                        acc = acc + taps[k] * regs[i + k]
                    act_ref[t, pl.ds(r0 + i, SUBLANES, stride=CONV_STRIDE), :] = _silu(acc)
            if is_x:
                for r0 in range(0, rows, CHUNK):
                    xsT_ref[0, cs, r0:r0 + CHUNK] = act_ref[t, r0:r0 + CHUNK, :].T.astype(BF16)
            else:
                bc_ref[:, pl.ds(slab * CONV_STRIP - D_SSD, CONV_STRIP)] = act_ref[t].astype(BF16)

    project_xbc(0)
    dtT_ref[...] = _dot_nt(wdtT_ref[...], h_own[...])

    n_steps = CONV_CH // XBC_STEP
    for j in range(n_steps):
        conv(j, j * XBC_STEP < D_SSD)
        if j + 1 < n_steps:
            project_xbc(j + 1)
        project_rest(j)


def _const_spec(shape):
    nd = len(shape)
    return pl.BlockSpec(shape, lambda *_: (0,) * nd, pipeline_mode=pl.Buffered(1))


def _layer_spec(stacked, layer):
    return pl.BlockSpec((None,) + stacked.shape[1:], lambda *_: (layer, 0, 0),
                        pipeline_mode=pl.Buffered(1))


def _in_proj(x, norm_g, w_xbc, w_rest, w_dtT, layer, conv_w, conv_b):
    bsz, seq, d = x.shape
    m = bsz * seq
    rows = IN_PROJ_ROWS
    tiles_per_seq = seq // rows
    n_tiles = m // rows
    halo_per_tile = rows // HALO_ROWS
    n_halo = m // HALO_ROWS
    x2 = x.reshape(m, d)

    def row_spec(width):
        return pl.BlockSpec((rows, width), lambda i: (i, 0))

    in_specs = [
        row_spec(d),
        pl.BlockSpec((HALO_ROWS, d), lambda i: (jnp.maximum(i * halo_per_tile - 1, 0), 0)),
        pl.BlockSpec((HALO_ROWS, d),
                     lambda i: (jnp.minimum((i + 1) * halo_per_tile, n_halo - 1), 0)),
        _const_spec((1, d)),
        _layer_spec(w_xbc, layer), _layer_spec(w_rest, layer), _layer_spec(w_dtT, layer),
        _const_spec(conv_w.shape), _const_spec((1, CONV_CH)),
    ]
    out_shape = [
        jax.ShapeDtypeStruct((m, REST_WIDTH), BF16),
        jax.ShapeDtypeStruct((bsz, D_SSD, seq), BF16),
        jax.ShapeDtypeStruct((m, 2 * BC_WIDTH), BF16),
        jax.ShapeDtypeStruct((2 * SSD_HEADS, m), F32),
    ]
    out_specs = [
        row_spec(REST_WIDTH),
        pl.BlockSpec((1, D_SSD, rows), lambda i: (i // tiles_per_seq, 0, i % tiles_per_seq)),
        row_spec(2 * BC_WIDTH),
        pl.BlockSpec((2 * SSD_HEADS, rows), lambda i: (0, i)),
    ]
    return pl.pallas_call(
        functools.partial(_in_proj_kernel, tiles_per_seq),
        grid=(n_tiles,),
        in_specs=in_specs,
        out_specs=out_specs,
        out_shape=out_shape,
        scratch_shapes=[pltpu.VMEM((CONV_CH // CONV_STRIP, rows + 2 * HALO_ROWS, CONV_STRIP), F32),
                        pltpu.VMEM((XBC_STEP // CONV_STRIP, rows, CONV_STRIP), F32),
                        pltpu.VMEM((rows + 2 * HALO_ROWS, d), BF16)],
        compiler_params=pltpu.CompilerParams(
            dimension_semantics=("arbitrary",), vmem_limit_bytes=VMEM_LIMIT_BYTES),
        name="in_proj",
    )(x2, x2, x2, norm_g.reshape(1, d), w_xbc, w_rest, w_dtT, conv_w, conv_b.reshape(1, CONV_CH))


def _expand_rows(v):
    lanes = v.shape[1]
    return jnp.concatenate(
        [jnp.broadcast_to(v[j:j + 1, :], (SSD_HEAD_DIM, lanes)) for j in range(v.shape[0])], axis=0)


(TAB_QF, TAB_QB, TAB_EF, TAB_EB, TAB_WF, TAB_WB, TAB_DECAY_F, TAB_DECAY_B, TAB_DTB,
 TAB_PF, TAB_PB) = range(11)
LOG2E = 1.4426950408889634


def _ssd_kernel(n_chunks,
                xsT_ref, b_ref, c_ref, dtT_ref, bias_col_ref, alog_col_ref, dskip_ref,
                y_ref,
                snap_ref, state_ref, tab_ref):
    H = SSD_HEADS
    row_id = lax.broadcasted_iota(jnp.int32, (CHUNK, CHUNK), 0)
    col_id = lax.broadcasted_iota(jnp.int32, (CHUNK, CHUNK), 1)
    lower = row_id >= col_id
    diag = row_id == col_id
    lower_f = lower.astype(F32)
    upper_f = (row_id <= col_id).astype(F32)

    dt_all = _softplus(dtT_ref[...] + bias_col_ref[...])
    a_all = dt_all * -jnp.exp(alog_col_ref[...])

    def stack(v, lo):
        return jnp.concatenate(
            [v[lo:lo + H, c * CHUNK:(c + 1) * CHUNK] for c in range(n_chunks)], axis=0)

    dt_f, dt_b = stack(dt_all, 0), stack(dt_all, H)
    cum_f = _dot_exact(stack(a_all, 0), upper_f)
    cum_b = _dot_exact(stack(a_all, H), lower_f)
    total_f = jnp.broadcast_to(cum_f[:, CHUNK - 1:CHUNK], cum_f.shape)
    total_b = jnp.broadcast_to(cum_b[:, 0:1], cum_b.shape)
    p_f = cum_f * LOG2E
    p_b = cum_b * LOG2E
    tab_ref[TAB_QF] = p_f - jnp.log2(dt_f)
    tab_ref[TAB_QB] = p_b - jnp.log2(dt_b)
    tab_ref[TAB_EF] = jnp.exp(cum_f)
    tab_ref[TAB_EB] = jnp.exp(cum_b)
    tab_ref[TAB_WF] = dt_f * jnp.exp(total_f - cum_f)
    tab_ref[TAB_WB] = dt_b * jnp.exp(total_b - cum_b)
    tab_ref[TAB_DECAY_F] = jnp.exp(total_f)
    tab_ref[TAB_DECAY_B] = jnp.exp(total_b)
    tab_ref[TAB_DTB] = dt_b
    tab_ref[TAB_PF] = p_f
    tab_ref[TAB_PB] = p_b
    def chunk_rows(k, c):
        return tab_ref[k, pl.ds(pl.multiple_of(c * H, H), H), :]

    def state_update(c, g, xs_t, w, decay):
        hs = slice(g * HEADS_PER_GROUP, (g + 1) * HEADS_PER_GROUP)
        rs = slice(g * GROUP_WIDTH, (g + 1) * GROUP_WIDTH)
        xd_t = (xs_t * _expand_rows(w[hs])).astype(BF16)
        b_g = b_ref[0, pl.ds(c * CHUNK, CHUNK), g * SSD_STATE:(g + 1) * SSD_STATE]
        state_ref[rs, :] = _expand_rows(decay[hs]) * state_ref[rs, :] + _dot(xd_t, b_g)

    state_ref[...] = jnp.zeros_like(state_ref)

    def fwd_body(c, carry):
        snap_ref[c] = state_ref[...].astype(BF16)
        w, decay = chunk_rows(TAB_WF, c), chunk_rows(TAB_DECAY_F, c)
        for g in range(SSD_GROUPS):
            rs = slice(g * GROUP_WIDTH, (g + 1) * GROUP_WIDTH)
            xs_t = xsT_ref[0, rs, pl.ds(c * CHUNK, CHUNK)].astype(F32)
            state_update(c, g, xs_t, w, decay)
        return carry

    lax.fori_loop(0, n_chunks, fwd_body, 0, unroll=math.gcd(n_chunks, FWD_UNROLL))

    state_ref[...] = jnp.zeros_like(state_ref)

    def bwd_body(it, carry):
        c = n_chunks - 1 - it
        q_f, q_b = chunk_rows(TAB_QF, c), chunk_rows(TAB_QB, c)
        e_f, e_b = chunk_rows(TAB_EF, c), chunk_rows(TAB_EB, c)
        w_b, decay_b = chunk_rows(TAB_WB, c), chunk_rows(TAB_DECAY_B, c)
        dt_b_c = chunk_rows(TAB_DTB, c)
        p_f_c, p_b_c = chunk_rows(TAB_PF, c), chunk_rows(TAB_PB, c)
        groups = []
        for g in range(SSD_GROUPS):
            hs = slice(g * HEADS_PER_GROUP, (g + 1) * HEADS_PER_GROUP)
            rs = slice(g * GROUP_WIDTH, (g + 1) * GROUP_WIDTH)
            b_g = b_ref[0, pl.ds(c * CHUNK, CHUNK), g * SSD_STATE:(g + 1) * SSD_STATE]
            c_g = c_ref[0, pl.ds(c * CHUNK, CHUNK), g * SSD_STATE:(g + 1) * SSD_STATE]
            xs_bf = xsT_ref[0, rs, pl.ds(c * CHUNK, CHUNK)]
            xs_t = xs_bf.astype(F32)
            cb = _dot_nt(c_g, b_g)
            states = jnp.concatenate([snap_ref[c, rs, :], state_ref[rs, :].astype(BF16)], axis=0)
            y_off = _dot_nt(states, c_g)
            state_update(c, g, xs_t, w_b, decay_b)
            groups.append((hs, rs, xs_bf, xs_t, cb, y_off))
        for g, (hs, rs, xs_bf, xs_t, cb, y_off) in enumerate(groups):
            y_parts = []
            for j in range(HEADS_PER_GROUP):
                hd = g * HEADS_PER_GROUP + j
                tgt_f = jnp.broadcast_to(p_f_c[hd:hd + 1, :], (CHUNK, CHUNK)).T
                tgt_b = jnp.broadcast_to(p_b_c[hd:hd + 1, :], (CHUNK, CHUNK)).T
                arg = jnp.where(lower, tgt_f - q_f[hd:hd + 1, :], tgt_b - q_b[hd:hd + 1, :])
                m_h = (cb * jnp.exp2(arg)).astype(BF16)
                y_parts.append(_dot_nt(xs_bf[j * SSD_HEAD_DIM:(j + 1) * SSD_HEAD_DIM, :], m_h))
            cb_diag = jnp.sum(jnp.where(diag, cb, 0.0), axis=0, keepdims=True)
            own = dskip_ref[rs, :] + _expand_rows(cb_diag * dt_b_c[hs])
            y_t = (jnp.concatenate(y_parts, axis=0)
                   + y_off[:GROUP_WIDTH] * _expand_rows(e_f[hs])
                   + y_off[GROUP_WIDTH:] * _expand_rows(e_b[hs])
                   + own * xs_t)
            y_ref[0, rs, pl.ds(c * CHUNK, CHUNK)] = y_t.astype(BF16)
        return carry

    lax.fori_loop(0, n_chunks, bwd_body, 0, unroll=math.gcd(n_chunks, BWD_UNROLL))


def _ssd(xsT, bc, dtT, dt_bias, a_log, d_skip):
    bsz, _, seq = xsT.shape
    n_chunks = seq // CHUNK
    two_h = 2 * SSD_HEADS
    bias_col = jnp.broadcast_to(dt_bias.reshape(two_h, 1).astype(F32), (two_h, seq))
    alog_col = jnp.broadcast_to(a_log.reshape(two_h, 1).astype(F32), (two_h, seq))
    dskip_col = jnp.broadcast_to(
        jnp.repeat(d_skip.astype(F32), SSD_HEAD_DIM)[:, None], (D_SSD, CHUNK))
    bc3 = bc.reshape(bsz, seq, 2 * BC_WIDTH)
    return pl.pallas_call(
        functools.partial(_ssd_kernel, n_chunks),
        grid=(bsz,),
        in_specs=[
            pl.BlockSpec((1, D_SSD, seq), lambda b: (b, 0, 0)),
            pl.BlockSpec((1, seq, BC_WIDTH), lambda b: (b, 0, 0)),
            pl.BlockSpec((1, seq, BC_WIDTH), lambda b: (b, 0, 1)),
            pl.BlockSpec((two_h, seq), lambda b: (0, b)),
            _const_spec((two_h, seq)), _const_spec((two_h, seq)),
            _const_spec((D_SSD, CHUNK)),
        ],
        out_specs=pl.BlockSpec((1, D_SSD, seq), lambda b: (b, 0, 0)),
        out_shape=jax.ShapeDtypeStruct((bsz, D_SSD, seq), BF16),
        scratch_shapes=[pltpu.VMEM((n_chunks, D_SSD, SSD_STATE), BF16),
                        pltpu.VMEM((D_SSD, SSD_STATE), F32),
                        pltpu.VMEM((TAB_PB + 1, n_chunks * SSD_HEADS, CHUNK), F32)],
        compiler_params=pltpu.CompilerParams(
            dimension_semantics=("arbitrary",), vmem_limit_bytes=VMEM_LIMIT_BYTES),
        name="ssd",
    )(xsT, bc3, bc3, dtT, bias_col, alog_col, dskip_col)


def _mix_kernel(apply_final_norm,
                x_ref, ys_ref, rest_ref, ssdg_ref, uf_ref,
                mem_ref, memg_ref, wkv_ref, chan_dft_ref, seq_dft_ref,
                pssd_ref, pfno_ref, pmem_ref, wout_ref, finalg_ref,
                out_ref,
                ab_ref, kv_ref, dft_ref, mid_ref):
    seq = uf_ref.shape[1]
    lt = pl.program_id(1)
    zf_lo = D_SSD + D_FNO
    q_lo = zf_lo + D_FNO
    zm_lo = q_lo + D_MEM
    gates_lo = zm_lo + D_MEM

    @pl.when(lt == 0)
    def _per_sequence():
        half = seq // 2
        r_id = lax.broadcasted_iota(jnp.int32, (CHUNK, 2 * CHUNK), 0)
        c_id = lax.broadcasted_iota(jnp.int32, (CHUNK, 2 * CHUNK), 1)
        reverse = (r_id + c_id == CHUNK).astype(BF16)
        sign = jnp.concatenate([jnp.ones((1, D_FNO), F32), -jnp.ones((1, D_FNO), F32)], axis=1)
        for gi in range(FNO_GROUPS):
            cs = slice(gi * FNO_GROUP_DIM, (gi + 1) * FNO_GROUP_DIM)
            t = _dot(uf_ref[0, :, cs], chan_dft_ref[...]).astype(BF16)
            dft_ref[:, cs] = t[:, :FNO_GROUP_DIM]
            dft_ref[:, D_FNO + gi * FNO_GROUP_DIM:D_FNO + (gi + 1) * FNO_GROUP_DIM] = (
                t[:, FNO_GROUP_DIM:])
        mid_ref[0:1, :] = dft_ref[half:half + 1, :D_FNO].astype(F32)
        for blk in range(half // CHUNK):
            lo = seq - (blk + 1) * CHUNK
            if blk == 0:
                window = jnp.concatenate(
                    [dft_ref[lo:, :], jnp.zeros((CHUNK, 2 * D_FNO), BF16)], axis=0)
            else:
                window = dft_ref[lo:lo + 2 * CHUNK, :]
            own = dft_ref[blk * CHUNK:(blk + 1) * CHUNK, :].astype(F32)
            folded = (own + sign * _dot(reverse, window)).astype(BF16)
            ab_ref[blk * CHUNK:(blk + 1) * CHUNK, :] = folded[:, :D_FNO]
            ab_ref[half + blk * CHUNK:half + (blk + 1) * CHUNK, :] = folded[:, D_FNO:]
        hm = (_rms_scale(mem_ref[0]) * memg_ref[...]).astype(BF16)
        kv_ref[...] = _dot(hm, wkv_ref[...]).astype(BF16)

    yf = _dot(seq_dft_ref[...], ab_ref[...])
    q = rest_ref[0, :, q_lo:zm_lo]
    scores = []
    for hd in range(MEM_HEADS):
        cs = slice(hd * MEM_HEAD_DIM, (hd + 1) * MEM_HEAD_DIM)
        scores.append(_dot_nt(q[:, cs], kv_ref[:, cs]))

    t = ys_ref[0].astype(F32).T * _silu(rest_ref[0, :, :D_SSD].astype(F32))
    parts = []
    for gi in range(SSD_GROUPS):
        parts.append(_rms_scale(t[:, gi * GROUP_WIDTH:(gi + 1) * GROUP_WIDTH]))
    ysn = (jnp.concatenate(parts, axis=1) * ssdg_ref[...]).astype(BF16)
    m_s = _dot(ysn, pssd_ref[...])

    k_id = lax.broadcasted_iota(jnp.int32, (x_ref.shape[1], D_FNO), 0)
    alt = jnp.where(k_id % 2 == 0, seq ** -0.5, -(seq ** -0.5))
    yf = (yf + alt * mid_ref[0:1, :]) * _silu(rest_ref[0, :, zf_lo:q_lo].astype(F32))
    m_f = _dot(yf.astype(BF16), pfno_ref[...])

    heads = []
    for hd in range(MEM_HEADS):
        s = scores[hd] * (MEM_HEAD_DIM ** -0.5)
        p = jnp.exp(s - jnp.max(s, axis=-1, keepdims=True))
        denom = jnp.sum(p, axis=-1, keepdims=True)
        pv = _dot(p.astype(BF16), kv_ref[:, D_MEM + hd * MEM_HEAD_DIM:D_MEM + (hd + 1) * MEM_HEAD_DIM])
        heads.append(pv / denom)
    ym = jnp.concatenate(heads, axis=1) * _silu(rest_ref[0, :, zm_lo:gates_lo].astype(F32))
    m_m = _dot(ym.astype(BF16), pmem_ref[...])

    gate = _sigmoid(rest_ref[0, :, gates_lo:].astype(F32))
    merged = (gate[:, :D_MODEL] * m_s + gate[:, D_MODEL:2 * D_MODEL] * m_f
              + gate[:, 2 * D_MODEL:] * m_m)
    out = x_ref[0] + _dot(merged.astype(BF16), wout_ref[...])
    if apply_final_norm:
        out = _rms_scale(out) * finalg_ref[...]
    out_ref[0] = out


def _mix(x, ys, rest, ssd_norm_g, mem, mem_norm_g, chan_dft, seq_dft, w_kv, p_ssd, p_fno, p_mem,
         w_out, layer, final_g, apply_final_norm):
    bsz, seq, d = x.shape
    rows = MIX_ROWS
    n_tiles = seq // rows
    rest3 = rest.reshape(bsz, seq, REST_WIDTH)

    def tile_spec(width):
        return pl.BlockSpec((1, rows, width), lambda b, t: (b, t, 0))

    assert D_SSD == 2 * D_FNO
    in_specs = [
        tile_spec(d), pl.BlockSpec((1, D_SSD, rows), lambda b, t: (b, 0, t)), tile_spec(REST_WIDTH),
        _const_spec((1, D_SSD)),
        pl.BlockSpec((1, seq, D_FNO), lambda b, t: (b, 0, 2)),
        pl.BlockSpec((1, N_MEM, d), lambda b, t: (b, 0, 0)),
        _const_spec((1, d)), _layer_spec(w_kv, layer), _const_spec(chan_dft.shape),
        pl.BlockSpec((rows, seq), lambda b, t: (t, 0)),
        _layer_spec(p_ssd, layer), _layer_spec(p_fno, layer), _layer_spec(p_mem, layer),
        _layer_spec(w_out, layer), _const_spec((1, d)),
    ]
    return pl.pallas_call(
        functools.partial(_mix_kernel, apply_final_norm),
        grid=(bsz, n_tiles),
        in_specs=in_specs,
        out_specs=tile_spec(d),
        out_shape=jax.ShapeDtypeStruct((bsz, seq, d), F32),
        scratch_shapes=[pltpu.VMEM((seq, D_FNO), BF16),
                        pltpu.VMEM((N_MEM, 2 * D_MEM), BF16),
                        pltpu.VMEM((seq, 2 * D_FNO), BF16),
                        pltpu.VMEM((SUBLANES, D_FNO), F32)],
        compiler_params=pltpu.CompilerParams(
            dimension_semantics=("arbitrary", "arbitrary"), vmem_limit_bytes=VMEM_LIMIT_BYTES),
        name="mix",
    )(x, ys, rest3, ssd_norm_g.reshape(1, D_SSD), rest3, mem,
      mem_norm_g.reshape(1, d), w_kv, chan_dft, seq_dft, p_ssd, p_fno, p_mem, w_out,
      final_g.reshape(1, d))


def _dft_table(n, split, sin_sign, n_cols):
    m = jnp.arange(n_cols, dtype=jnp.int32)
    k_hi = jnp.arange(n // split, dtype=jnp.int32) * split
    k_lo = jnp.arange(split, dtype=jnp.int32)
    ang_hi = (2.0 * math.pi / n) * ((k_hi[:, None] * m[None, :]) % n).astype(F32)
    ang_lo = (2.0 * math.pi / n) * ((k_lo[:, None] * m[None, :]) % n).astype(F32)
    ca, sa = jnp.cos(ang_hi), jnp.sin(ang_hi)
    cb, sb = jnp.cos(ang_lo), jnp.sin(ang_lo)
    u = jnp.concatenate([ca, sin_sign * sa], axis=1)[:, None, :]
    v = jnp.concatenate([-sa, sin_sign * ca], axis=1)[:, None, :]
    cb2 = jnp.concatenate([cb, cb], axis=1)[None, :, :]
    sb2 = jnp.concatenate([sb, sb], axis=1)[None, :, :]
    return ((u * cb2 + v * sb2) * (n ** -0.5)).astype(BF16).reshape(n, 2 * n_cols)


def _dft_tables(seq):
    return (_dft_table(FNO_GROUP_DIM, 8, 1.0, FNO_GROUP_DIM),
            _dft_table(seq, 64, -1.0, seq // 2))


def _transpose_cast_kernel(tail_rows, a_ref, b_ref, o_ref):
    a = a_ref[0]
    shifted = jnp.concatenate([a[tail_rows:], b_ref[0]], axis=0)
    rows = jnp.where(pl.program_id(1) == 0, a, shifted) if tail_rows else a
    o_ref[0] = rows.T.astype(BF16)


def _split_w_in(w_in):
    depth, d, _ = w_in.shape
    w_t = jnp.swapaxes(w_in, 1, 2)
    step = W_SPLIT_COLS
    tail = IN_SIZES[2]
    xbc_lo, dt_lo = IN_SIZES[0], IN_SIZES[0] + IN_SIZES[1]
    assert xbc_lo == step and dt_lo % step == 0 and step % tail == 0

    def call(n_steps, a_map, b_map, tail_rows, name):
        return pl.pallas_call(
            functools.partial(_transpose_cast_kernel, tail_rows),
            grid=(depth, n_steps),
            in_specs=[pl.BlockSpec((1, step, d), a_map), pl.BlockSpec((1, tail, d), b_map)],
            out_specs=pl.BlockSpec((1, d, step), lambda i, c: (i, 0, c)),
            out_shape=jax.ShapeDtypeStruct((depth, d, n_steps * step), BF16),
            compiler_params=pltpu.CompilerParams(
                dimension_semantics=("arbitrary", "arbitrary"),
                vmem_limit_bytes=VMEM_LIMIT_BYTES),
            name=name,
        )(w_t, w_t)

    first_rest = dt_lo // step
    w_xbc = call(CONV_CH // step, lambda i, c: (i, xbc_lo // step + c, 0),
                 lambda i, c: (i, 0, 0), 0, "split_w_xbc")
    w_rest = call(REST_WIDTH // step,
                  lambda i, c: (i, jnp.where(c == 0, 0, first_rest - 1 + c), 0),
                  lambda i, c: (i, (first_rest + c) * (step // tail), 0), tail, "split_w_rest")
    w_dt_t = w_t[:, dt_lo:dt_lo + tail, :].astype(BF16)
    return w_xbc, w_rest, w_dt_t


def _layer(x, mem, layer, norm_g, w_in_parts, conv_w, conv_b, dt_bias, a_log, d_skip, ssd_norm_g,
           mem_norm_g, mix_weights, final_g, tables, apply_final_norm):
    w_xbc, w_rest, w_dtT = w_in_parts
    rest, xsT, bc, dtT = _in_proj(x, norm_g, w_xbc, w_rest, w_dtT, layer, conv_w, conv_b)
    ys = _ssd(xsT, bc, dtT, dt_bias, a_log, d_skip)
    return _mix(x, ys, rest, ssd_norm_g, mem, mem_norm_g, *tables, *mix_weights, layer, final_g,
                apply_final_norm)


def kernel(x, mem, norm_g, w_in, conv_w, conv_b, dt_bias, a_log, d_skip, ssd_norm_g, mem_norm_g,
           w_kv, p_ssd, p_fno, p_mem, w_out, final_g):
    depth = norm_g.shape[0]
    tables = _dft_tables(x.shape[1])
    w_in_parts = _split_w_in(w_in)
    mix_weights = tuple(w.astype(BF16) for w in (w_kv, p_ssd, p_fno, p_mem, w_out))
    for i in range(depth):
        x = _layer(x, mem, i, norm_g[i], w_in_parts, conv_w[i], conv_b[i], dt_bias[i], a_log[i],
                   d_skip[i], ssd_norm_g[i], mem_norm_g[i], mix_weights, final_g, tables,
                   i == depth - 1)
    return x
```

```python
import functools
import math

import jax
import jax.numpy as jnp
from jax import lax
from jax.experimental import pallas as pl
from jax.experimental.pallas import tpu as pltpu

F32 = jnp.float32
BF16 = jnp.bfloat16
HIGHEST = lax.Precision.HIGHEST

D_MODEL = 1024
D_SSD = 1024
SSD_HEAD_DIM = 64
SSD_HEADS = D_SSD // SSD_HEAD_DIM
SSD_GROUPS = 4
HEADS_PER_GROUP = SSD_HEADS // SSD_GROUPS
GROUP_WIDTH = HEADS_PER_GROUP * SSD_HEAD_DIM
SSD_STATE = 128
BC_WIDTH = SSD_GROUPS * SSD_STATE
D_CONV = 5
CONV_CH = D_SSD + 2 * BC_WIDTH
CHUNK = 128
D_FNO = 512
FNO_GROUPS = 4
FNO_GROUP_DIM = D_FNO // FNO_GROUPS
N_MEM = 256
MEM_HEADS = 4
D_MEM = 512
MEM_HEAD_DIM = D_MEM // MEM_HEADS
N_BRANCH = 3
EPS = 1e-6
IN_SIZES = (D_SSD, CONV_CH, 2 * SSD_HEADS, D_FNO, D_FNO, D_MEM, D_MEM, N_BRANCH * D_MODEL)
REST_WIDTH = sum(IN_SIZES) - CONV_CH - 2 * SSD_HEADS
XBC_STEP = 256
REST_STEP = REST_WIDTH // (CONV_CH // XBC_STEP)

SUBLANES = 8
HALO_ROWS = 16
CONV_STRIDE = 4
CONV_STRIP = 128
CONV_HALO = D_CONV // 2
IN_PROJ_ROWS = 512
MIX_ROWS = 512
W_SPLIT_COLS = 1024
FWD_UNROLL = 16
BWD_UNROLL = 16
MIB = 1024 * 1024
VMEM_LIMIT_SPLIT = 24 * MIB
VMEM_LIMIT_IN_PROJ = 48 * MIB
VMEM_LIMIT_SSD = 44 * MIB
VMEM_LIMIT_MIX = 56 * MIB


def _sigmoid(v):
    return 0.5 * jnp.tanh(0.5 * v) + 0.5


def _silu(v):
    h = 0.5 * v
    return h * (jnp.tanh(h) + 1.0)


def _softplus(v):
    return jnp.maximum(v, 0.0) + jnp.log1p(jnp.exp(-jnp.abs(v)))


def _rms_scale(v):
    return v * lax.rsqrt(jnp.mean(v * v, axis=-1, keepdims=True) + EPS)


def _dot(a, b):
    return jnp.dot(a, b, preferred_element_type=F32)


def _dot_nt(a, b):
    return lax.dot_general(a, b, (((1,), (1,)), ((), ())), preferred_element_type=F32)


def _dot_exact(a, b):
    return jnp.dot(a, b, precision=HIGHEST, preferred_element_type=F32)


def _in_proj_kernel(tiles_per_seq,
                    x_ref, xprev_ref, xnext_ref, g_ref,
                    wxbc_ref, wrest_ref, wdtT_ref, convw_ref, convb_ref,
                    rest_ref, xsT_ref, bc_ref, dtT_ref,
                    stage_ref, act_ref, h_ref):
    rows = x_ref.shape[0]
    lt = pl.program_id(0) % tiles_per_seq
    g = g_ref[...]
    keep_prev = (lt > 0).astype(F32)
    keep_next = (lt < tiles_per_seq - 1).astype(F32)
    h_ref[0:HALO_ROWS, :] = (_rms_scale(xprev_ref[...]) * g * keep_prev).astype(BF16)
    h_ref[HALO_ROWS:HALO_ROWS + rows, :] = (_rms_scale(x_ref[...]) * g).astype(BF16)
    h_ref[HALO_ROWS + rows:, :] = (_rms_scale(xnext_ref[...]) * g * keep_next).astype(BF16)
    h_own = h_ref.at[HALO_ROWS:HALO_ROWS + rows, :]

    strips_per_step = XBC_STEP // CONV_STRIP

    def project_xbc(j):
        res = _dot(h_ref[...], wxbc_ref[:, pl.ds(j * XBC_STEP, XBC_STEP)])
        for t in range(strips_per_step):
            stage_ref[j * strips_per_step + t] = res[:, t * CONV_STRIP:(t + 1) * CONV_STRIP]

    def project_rest(j):
        cols = pl.ds(j * REST_STEP, REST_STEP)
        rest_ref[:, cols] = _dot(h_own[...], wrest_ref[:, cols]).astype(BF16)

    def conv(j, is_x):
        block = SUBLANES * CONV_STRIDE
        for t in range(strips_per_step):
            slab = j * strips_per_step + t
            cs = pl.ds(slab * CONV_STRIP, CONV_STRIP)
            bias = convb_ref[:, cs]
            taps = [convw_ref[k:k + 1, cs] for k in range(D_CONV)]
            for r0 in range(0, rows, block):
                base = HALO_ROWS + r0 - CONV_HALO
                regs = [stage_ref[slab, pl.ds(base + m, SUBLANES, stride=CONV_STRIDE), :]
                        for m in range(CONV_STRIDE + 2 * CONV_HALO)]
                for i in range(CONV_STRIDE):
                    acc = bias + taps[0] * regs[i]
                    for k in range(1, D_CONV):
                        acc = acc + taps[k] * regs[i + k]
                    act_ref[t, pl.ds(r0 + i, SUBLANES, stride=CONV_STRIDE), :] = _silu(acc)
            if is_x:
                for r0 in range(0, rows, CHUNK):
                    xsT_ref[0, cs, r0:r0 + CHUNK] = act_ref[t, r0:r0 + CHUNK, :].T.astype(BF16)
            else:
                bc_ref[:, pl.ds(slab * CONV_STRIP - D_SSD, CONV_STRIP)] = act_ref[t].astype(BF16)

    project_xbc(0)
    dtT_ref[...] = _dot_nt(wdtT_ref[...], h_own[...])

    n_steps = CONV_CH // XBC_STEP
    for j in range(n_steps):
        conv(j, j * XBC_STEP < D_SSD)
        if j + 1 < n_steps:
            project_xbc(j + 1)
        project_rest(j)


def _const_spec(shape):
    nd = len(shape)
    return pl.BlockSpec(shape, lambda *_: (0,) * nd, pipeline_mode=pl.Buffered(1))


def _layer_spec(stacked, layer):
    return pl.BlockSpec((None,) + stacked.shape[1:], lambda *_: (layer, 0, 0),
                        pipeline_mode=pl.Buffered(1))


def _in_proj(x, norm_g, w_xbc, w_rest, w_dtT, layer, conv_w, conv_b):
    bsz, seq, d = x.shape
    m = bsz * seq
    rows = IN_PROJ_ROWS
    tiles_per_seq = seq // rows
    n_tiles = m // rows
    halo_per_tile = rows // HALO_ROWS
    n_halo = m // HALO_ROWS
    x2 = x.reshape(m, d)

    def row_spec(width):
        return pl.BlockSpec((rows, width), lambda i: (i, 0))

    in_specs = [
        row_spec(d),
        pl.BlockSpec((HALO_ROWS, d), lambda i: (jnp.maximum(i * halo_per_tile - 1, 0), 0)),
        pl.BlockSpec((HALO_ROWS, d),
                     lambda i: (jnp.minimum((i + 1) * halo_per_tile, n_halo - 1), 0)),
        _const_spec((1, d)),
        _layer_spec(w_xbc, layer), _layer_spec(w_rest, layer), _layer_spec(w_dtT, layer),
        _const_spec(conv_w.shape), _const_spec((1, CONV_CH)),
    ]
    out_shape = [
        jax.ShapeDtypeStruct((m, REST_WIDTH), BF16),
        jax.ShapeDtypeStruct((bsz, D_SSD, seq), BF16),
        jax.ShapeDtypeStruct((m, 2 * BC_WIDTH), BF16),
        jax.ShapeDtypeStruct((2 * SSD_HEADS, m), F32),
    ]
    out_specs = [
        row_spec(REST_WIDTH),
        pl.BlockSpec((1, D_SSD, rows), lambda i: (i // tiles_per_seq, 0, i % tiles_per_seq)),
        row_spec(2 * BC_WIDTH),
        pl.BlockSpec((2 * SSD_HEADS, rows), lambda i: (0, i)),
    ]
    return pl.pallas_call(
        functools.partial(_in_proj_kernel, tiles_per_seq),
        grid=(n_tiles,),
        in_specs=in_specs,
        out_specs=out_specs,
        out_shape=out_shape,
        scratch_shapes=[pltpu.VMEM((CONV_CH // CONV_STRIP, rows + 2 * HALO_ROWS, CONV_STRIP), F32),
                        pltpu.VMEM((XBC_STEP // CONV_STRIP, rows, CONV_STRIP), F32),
                        pltpu.VMEM((rows + 2 * HALO_ROWS, d), BF16)],
        compiler_params=pltpu.CompilerParams(
            dimension_semantics=("parallel",), vmem_limit_bytes=VMEM_LIMIT_IN_PROJ),
        name="in_proj",
    )(x2, x2, x2, norm_g.reshape(1, d), w_xbc, w_rest, w_dtT, conv_w, conv_b.reshape(1, CONV_CH))


def _expand_rows(v):
    lanes = v.shape[1]
    return jnp.concatenate(
        [jnp.broadcast_to(v[j:j + 1, :], (SSD_HEAD_DIM, lanes)) for j in range(v.shape[0])], axis=0)


(TAB_QF, TAB_QB, TAB_EF, TAB_EB, TAB_WF, TAB_WB, TAB_DECAY_F, TAB_DECAY_B, TAB_DTB,
 TAB_PF, TAB_PB) = range(11)
LOG2E = 1.4426950408889634


def _ssd_kernel(n_chunks,
                xsT_ref, b_ref, c_ref, dtT_ref, bias_col_ref, alog_col_ref, dskip_ref,
                y_ref,
                snap_ref, state_ref, tab_ref):
    H = SSD_HEADS
    row_id = lax.broadcasted_iota(jnp.int32, (CHUNK, CHUNK), 0)
    col_id = lax.broadcasted_iota(jnp.int32, (CHUNK, CHUNK), 1)
    lower = row_id >= col_id
    diag = row_id == col_id
    lower_f = lower.astype(F32)
    upper_f = (row_id <= col_id).astype(F32)

    dt_all = _softplus(dtT_ref[...] + bias_col_ref[...])
    a_all = dt_all * -jnp.exp(alog_col_ref[...])

    def stack(v, lo):
        return jnp.concatenate(
            [v[lo:lo + H, c * CHUNK:(c + 1) * CHUNK] for c in range(n_chunks)], axis=0)

    dt_f, dt_b = stack(dt_all, 0), stack(dt_all, H)
    cum_f = _dot_exact(stack(a_all, 0), upper_f)
    cum_b = _dot_exact(stack(a_all, H), lower_f)
    total_f = jnp.broadcast_to(cum_f[:, CHUNK - 1:CHUNK], cum_f.shape)
    total_b = jnp.broadcast_to(cum_b[:, 0:1], cum_b.shape)
    p_f = cum_f * LOG2E
    p_b = cum_b * LOG2E
    tab_ref[TAB_QF] = p_f - jnp.log2(dt_f)
    tab_ref[TAB_QB] = p_b - jnp.log2(dt_b)
    tab_ref[TAB_EF] = jnp.exp(cum_f)
    tab_ref[TAB_EB] = jnp.exp(cum_b)
    tab_ref[TAB_WF] = dt_f * jnp.exp(total_f - cum_f)
    tab_ref[TAB_WB] = dt_b * jnp.exp(total_b - cum_b)
    tab_ref[TAB_DECAY_F] = jnp.exp(total_f)
    tab_ref[TAB_DECAY_B] = jnp.exp(total_b)
    tab_ref[TAB_DTB] = dt_b
    tab_ref[TAB_PF] = p_f
    tab_ref[TAB_PB] = p_b
    def chunk_rows(k, c):
        return tab_ref[k, pl.ds(pl.multiple_of(c * H, H), H), :]

    def state_update(c, g, xs_t, w, decay):
        hs = slice(g * HEADS_PER_GROUP, (g + 1) * HEADS_PER_GROUP)
        rs = slice(g * GROUP_WIDTH, (g + 1) * GROUP_WIDTH)
        xd_t = (xs_t * _expand_rows(w[hs])).astype(BF16)
        b_g = b_ref[0, pl.ds(c * CHUNK, CHUNK), g * SSD_STATE:(g + 1) * SSD_STATE]
        state_ref[rs, :] = _expand_rows(decay[hs]) * state_ref[rs, :] + _dot(xd_t, b_g)

    state_ref[...] = jnp.zeros_like(state_ref)

    def fwd_body(c, carry):
        snap_ref[c] = state_ref[...].astype(BF16)
        w, decay = chunk_rows(TAB_WF, c), chunk_rows(TAB_DECAY_F, c)
        for g in range(SSD_GROUPS):
            rs = slice(g * GROUP_WIDTH, (g + 1) * GROUP_WIDTH)
            xs_t = xsT_ref[0, rs, pl.ds(c * CHUNK, CHUNK)].astype(F32)
            state_update(c, g, xs_t, w, decay)
        return carry

    lax.fori_loop(0, n_chunks, fwd_body, 0, unroll=math.gcd(n_chunks, FWD_UNROLL))

    state_ref[...] = jnp.zeros_like(state_ref)

    def bwd_body(it, carry):
        c = n_chunks - 1 - it
        q_f, q_b = chunk_rows(TAB_QF, c), chunk_rows(TAB_QB, c)
        e_f, e_b = chunk_rows(TAB_EF, c), chunk_rows(TAB_EB, c)
        w_b, decay_b = chunk_rows(TAB_WB, c), chunk_rows(TAB_DECAY_B, c)
        dt_b_c = chunk_rows(TAB_DTB, c)
        p_f_c, p_b_c = chunk_rows(TAB_PF, c), chunk_rows(TAB_PB, c)
        groups = []
        for g in range(SSD_GROUPS):
            hs = slice(g * HEADS_PER_GROUP, (g + 1) * HEADS_PER_GROUP)
            rs = slice(g * GROUP_WIDTH, (g + 1) * GROUP_WIDTH)
            b_g = b_ref[0, pl.ds(c * CHUNK, CHUNK), g * SSD_STATE:(g + 1) * SSD_STATE]
            c_g = c_ref[0, pl.ds(c * CHUNK, CHUNK), g * SSD_STATE:(g + 1) * SSD_STATE]
            xs_bf = xsT_ref[0, rs, pl.ds(c * CHUNK, CHUNK)]
            xs_t = xs_bf.astype(F32)
            cb = _dot_nt(c_g, b_g)
            states = jnp.concatenate([snap_ref[c, rs, :], state_ref[rs, :].astype(BF16)], axis=0)
            y_off = _dot_nt(states, c_g)
            state_update(c, g, xs_t, w_b, decay_b)
            groups.append((hs, rs, xs_bf, xs_t, cb, y_off))
        for g, (hs, rs, xs_bf, xs_t, cb, y_off) in enumerate(groups):
            y_parts = []
            for j in range(HEADS_PER_GROUP):
                hd = g * HEADS_PER_GROUP + j
                tgt_f = jnp.broadcast_to(p_f_c[hd:hd + 1, :], (CHUNK, CHUNK)).T
                tgt_b = jnp.broadcast_to(p_b_c[hd:hd + 1, :], (CHUNK, CHUNK)).T
                arg = jnp.where(lower, tgt_f - q_f[hd:hd + 1, :], tgt_b - q_b[hd:hd + 1, :])
                m_h = (cb * jnp.exp2(arg)).astype(BF16)
                y_parts.append(_dot_nt(xs_bf[j * SSD_HEAD_DIM:(j + 1) * SSD_HEAD_DIM, :], m_h))
            cb_diag = jnp.sum(jnp.where(diag, cb, 0.0), axis=0, keepdims=True)
            own = dskip_ref[rs, :] + _expand_rows(cb_diag * dt_b_c[hs])
            y_t = (jnp.concatenate(y_parts, axis=0)
                   + y_off[:GROUP_WIDTH] * _expand_rows(e_f[hs])
                   + y_off[GROUP_WIDTH:] * _expand_rows(e_b[hs])
                   + own * xs_t)
            y_ref[0, rs, pl.ds(c * CHUNK, CHUNK)] = y_t.astype(BF16)
        return carry

    lax.fori_loop(0, n_chunks, bwd_body, 0, unroll=math.gcd(n_chunks, BWD_UNROLL))


def _ssd(xsT, bc, dtT, dt_bias, a_log, d_skip):
    bsz, _, seq = xsT.shape
    n_chunks = seq // CHUNK
    two_h = 2 * SSD_HEADS
    bias_col = jnp.broadcast_to(dt_bias.reshape(two_h, 1).astype(F32), (two_h, seq))
    alog_col = jnp.broadcast_to(a_log.reshape(two_h, 1).astype(F32), (two_h, seq))
    dskip_col = jnp.broadcast_to(
        jnp.repeat(d_skip.astype(F32), SSD_HEAD_DIM)[:, None], (D_SSD, CHUNK))
    bc3 = bc.reshape(bsz, seq, 2 * BC_WIDTH)
    return pl.pallas_call(
        functools.partial(_ssd_kernel, n_chunks),
        grid=(bsz,),
        in_specs=[
            pl.BlockSpec((1, D_SSD, seq), lambda b: (b, 0, 0)),
            pl.BlockSpec((1, seq, BC_WIDTH), lambda b: (b, 0, 0)),
            pl.BlockSpec((1, seq, BC_WIDTH), lambda b: (b, 0, 1)),
            pl.BlockSpec((two_h, seq), lambda b: (0, b)),
            _const_spec((two_h, seq)), _const_spec((two_h, seq)),
            _const_spec((D_SSD, CHUNK)),
        ],
        out_specs=pl.BlockSpec((1, D_SSD, seq), lambda b: (b, 0, 0)),
        out_shape=jax.ShapeDtypeStruct((bsz, D_SSD, seq), BF16),
        scratch_shapes=[pltpu.VMEM((n_chunks, D_SSD, SSD_STATE), BF16),
                        pltpu.VMEM((D_SSD, SSD_STATE), F32),
                        pltpu.VMEM((TAB_PB + 1, n_chunks * SSD_HEADS, CHUNK), F32)],
        compiler_params=pltpu.CompilerParams(
            dimension_semantics=("parallel",), vmem_limit_bytes=VMEM_LIMIT_SSD),
        name="ssd",
    )(xsT, bc3, bc3, dtT, bias_col, alog_col, dskip_col)


def _mix_kernel(apply_final_norm,
                x_ref, ys_ref, rest_ref, ssdg_ref, uf_ref,
                mem_ref, memg_ref, wkv_ref, chan_dft_ref, seq_dft_ref,
                pssd_ref, pfno_ref, pmem_ref, wout_ref, finalg_ref,
                out_ref,
                ab_ref, kv_ref, dft_ref, mid_ref):
    seq = uf_ref.shape[1]
    lt = pl.program_id(1)
    zf_lo = D_SSD + D_FNO
    q_lo = zf_lo + D_FNO
    zm_lo = q_lo + D_MEM
    gates_lo = zm_lo + D_MEM

    @pl.when(lt == 0)
    def _per_sequence():
        half = seq // 2
        r_id = lax.broadcasted_iota(jnp.int32, (CHUNK, 2 * CHUNK), 0)
        c_id = lax.broadcasted_iota(jnp.int32, (CHUNK, 2 * CHUNK), 1)
        reverse = (r_id + c_id == CHUNK).astype(BF16)
        sign = jnp.concatenate([jnp.ones((1, D_FNO), F32), -jnp.ones((1, D_FNO), F32)], axis=1)
        for gi in range(FNO_GROUPS):
            cs = slice(gi * FNO_GROUP_DIM, (gi + 1) * FNO_GROUP_DIM)
            t = _dot(uf_ref[0, :, cs], chan_dft_ref[...]).astype(BF16)
            dft_ref[:, cs] = t[:, :FNO_GROUP_DIM]
            dft_ref[:, D_FNO + gi * FNO_GROUP_DIM:D_FNO + (gi + 1) * FNO_GROUP_DIM] = (
                t[:, FNO_GROUP_DIM:])
        mid_ref[0:1, :] = dft_ref[half:half + 1, :D_FNO].astype(F32)
        for blk in range(half // CHUNK):
            lo = seq - (blk + 1) * CHUNK
            if blk == 0:
                window = jnp.concatenate(
                    [dft_ref[lo:, :], jnp.zeros((CHUNK, 2 * D_FNO), BF16)], axis=0)
            else:
                window = dft_ref[lo:lo + 2 * CHUNK, :]
            own = dft_ref[blk * CHUNK:(blk + 1) * CHUNK, :].astype(F32)
            folded = (own + sign * _dot(reverse, window)).astype(BF16)
            ab_ref[blk * CHUNK:(blk + 1) * CHUNK, :] = folded[:, :D_FNO]
            ab_ref[half + blk * CHUNK:half + (blk + 1) * CHUNK, :] = folded[:, D_FNO:]
        hm = (_rms_scale(mem_ref[0]) * memg_ref[...]).astype(BF16)
        kv_ref[...] = _dot(hm, wkv_ref[...]).astype(BF16)

    yf = _dot(seq_dft_ref[...], ab_ref[...])
    q = rest_ref[0, :, q_lo:zm_lo]
    scores = []
    for hd in range(MEM_HEADS):
        cs = slice(hd * MEM_HEAD_DIM, (hd + 1) * MEM_HEAD_DIM)
        scores.append(_dot_nt(q[:, cs], kv_ref[:, cs]))

    t = ys_ref[0].astype(F32).T * _silu(rest_ref[0, :, :D_SSD].astype(F32))
    parts = []
    for gi in range(SSD_GROUPS):
        parts.append(_rms_scale(t[:, gi * GROUP_WIDTH:(gi + 1) * GROUP_WIDTH]))
    ysn = (jnp.concatenate(parts, axis=1) * ssdg_ref[...]).astype(BF16)
    m_s = _dot(ysn, pssd_ref[...])

    k_id = lax.broadcasted_iota(jnp.int32, (x_ref.shape[1], D_FNO), 0)
    alt = jnp.where(k_id % 2 == 0, seq ** -0.5, -(seq ** -0.5))
    yf = (yf + alt * mid_ref[0:1, :]) * _silu(rest_ref[0, :, zf_lo:q_lo].astype(F32))
    m_f = _dot(yf.astype(BF16), pfno_ref[...])

    heads = []
    for hd in range(MEM_HEADS):
        s = scores[hd] * (MEM_HEAD_DIM ** -0.5)
        p = jnp.exp(s - jnp.max(s, axis=-1, keepdims=True))
        denom = jnp.sum(p, axis=-1, keepdims=True)
        pv = _dot(p.astype(BF16), kv_ref[:, D_MEM + hd * MEM_HEAD_DIM:D_MEM + (hd + 1) * MEM_HEAD_DIM])
        heads.append(pv / denom)
    ym = jnp.concatenate(heads, axis=1) * _silu(rest_ref[0, :, zm_lo:gates_lo].astype(F32))
    m_m = _dot(ym.astype(BF16), pmem_ref[...])

    gate = _sigmoid(rest_ref[0, :, gates_lo:].astype(F32))
    merged = (gate[:, :D_MODEL] * m_s + gate[:, D_MODEL:2 * D_MODEL] * m_f
              + gate[:, 2 * D_MODEL:] * m_m)
    out = x_ref[0] + _dot(merged.astype(BF16), wout_ref[...])
    if apply_final_norm:
        out = _rms_scale(out) * finalg_ref[...]
    out_ref[0] = out


def _mix(x, ys, rest, ssd_norm_g, mem, mem_norm_g, chan_dft, seq_dft, w_kv, p_ssd, p_fno, p_mem,
         w_out, layer, final_g, apply_final_norm):
    bsz, seq, d = x.shape
    rows = MIX_ROWS
    n_tiles = seq // rows
    rest3 = rest.reshape(bsz, seq, REST_WIDTH)

    def tile_spec(width):
        return pl.BlockSpec((1, rows, width), lambda b, t: (b, t, 0))

    assert D_SSD == 2 * D_FNO
    in_specs = [
        tile_spec(d), pl.BlockSpec((1, D_SSD, rows), lambda b, t: (b, 0, t)), tile_spec(REST_WIDTH),
        _const_spec((1, D_SSD)),
        pl.BlockSpec((1, seq, D_FNO), lambda b, t: (b, 0, 2)),
        pl.BlockSpec((1, N_MEM, d), lambda b, t: (b, 0, 0)),
        _const_spec((1, d)), _layer_spec(w_kv, layer), _const_spec(chan_dft.shape),
        pl.BlockSpec((rows, seq), lambda b, t: (t, 0)),
        _layer_spec(p_ssd, layer), _layer_spec(p_fno, layer), _layer_spec(p_mem, layer),
        _layer_spec(w_out, layer), _const_spec((1, d)),
    ]
    return pl.pallas_call(
        functools.partial(_mix_kernel, apply_final_norm),
        grid=(bsz, n_tiles),
        in_specs=in_specs,
        out_specs=tile_spec(d),
        out_shape=jax.ShapeDtypeStruct((bsz, seq, d), F32),
        scratch_shapes=[pltpu.VMEM((seq, D_FNO), BF16),
                        pltpu.VMEM((N_MEM, 2 * D_MEM), BF16),
                        pltpu.VMEM((seq, 2 * D_FNO), BF16),
                        pltpu.VMEM((SUBLANES, D_FNO), F32)],
        compiler_params=pltpu.CompilerParams(
            dimension_semantics=("parallel", "arbitrary"), vmem_limit_bytes=VMEM_LIMIT_MIX),
        name="mix",
    )(x, ys, rest3, ssd_norm_g.reshape(1, D_SSD), rest3, mem,
      mem_norm_g.reshape(1, d), w_kv, chan_dft, seq_dft, p_ssd, p_fno, p_mem, w_out,
      final_g.reshape(1, d))


def _dft_table(n, split, sin_sign, n_cols):
    m = jnp.arange(n_cols, dtype=jnp.int32)
    k_hi = jnp.arange(n // split, dtype=jnp.int32) * split
    k_lo = jnp.arange(split, dtype=jnp.int32)
    ang_hi = (2.0 * math.pi / n) * ((k_hi[:, None] * m[None, :]) % n).astype(F32)
    ang_lo = (2.0 * math.pi / n) * ((k_lo[:, None] * m[None, :]) % n).astype(F32)
    ca, sa = jnp.cos(ang_hi), jnp.sin(ang_hi)
    cb, sb = jnp.cos(ang_lo), jnp.sin(ang_lo)
    u = jnp.concatenate([ca, sin_sign * sa], axis=1)[:, None, :]
    v = jnp.concatenate([-sa, sin_sign * ca], axis=1)[:, None, :]
    cb2 = jnp.concatenate([cb, cb], axis=1)[None, :, :]
    sb2 = jnp.concatenate([sb, sb], axis=1)[None, :, :]
    return ((u * cb2 + v * sb2) * (n ** -0.5)).astype(BF16).reshape(n, 2 * n_cols)


def _dft_tables(seq):
    return (_dft_table(FNO_GROUP_DIM, 8, 1.0, FNO_GROUP_DIM),
            _dft_table(seq, 64, -1.0, seq // 2))


def _transpose_cast_kernel(tail_rows, a_ref, b_ref, o_ref):
    a = a_ref[0]
    shifted = jnp.concatenate([a[tail_rows:], b_ref[0]], axis=0)
    rows = jnp.where(pl.program_id(1) == 0, a, shifted) if tail_rows else a
    o_ref[0] = rows.T.astype(BF16)


def _split_w_in(w_in):
    depth, d, _ = w_in.shape
    w_t = jnp.swapaxes(w_in, 1, 2)
    step = W_SPLIT_COLS
    tail = IN_SIZES[2]
    xbc_lo, dt_lo = IN_SIZES[0], IN_SIZES[0] + IN_SIZES[1]
    assert xbc_lo == step and dt_lo % step == 0 and step % tail == 0

    def call(n_steps, a_map, b_map, tail_rows, name):
        return pl.pallas_call(
            functools.partial(_transpose_cast_kernel, tail_rows),
            grid=(depth, n_steps),
            in_specs=[pl.BlockSpec((1, step, d), a_map), pl.BlockSpec((1, tail, d), b_map)],
            out_specs=pl.BlockSpec((1, d, step), lambda i, c: (i, 0, c)),
            out_shape=jax.ShapeDtypeStruct((depth, d, n_steps * step), BF16),
            compiler_params=pltpu.CompilerParams(
                dimension_semantics=("parallel", "parallel"),
                vmem_limit_bytes=VMEM_LIMIT_SPLIT),
            name=name,
        )(w_t, w_t)

    first_rest = dt_lo // step
    w_xbc = call(CONV_CH // step, lambda i, c: (i, xbc_lo // step + c, 0),
                 lambda i, c: (i, 0, 0), 0, "split_w_xbc")
    w_rest = call(REST_WIDTH // step,
                  lambda i, c: (i, jnp.where(c == 0, 0, first_rest - 1 + c), 0),
                  lambda i, c: (i, (first_rest + c) * (step // tail), 0), tail, "split_w_rest")
    w_dt_t = w_t[:, dt_lo:dt_lo + tail, :].astype(BF16)
    return w_xbc, w_rest, w_dt_t


def _layer(x, mem, layer, norm_g, w_in_parts, conv_w, conv_b, dt_bias, a_log, d_skip, ssd_norm_g,
           mem_norm_g, mix_weights, final_g, tables, apply_final_norm):
    w_xbc, w_rest, w_dtT = w_in_parts
    rest, xsT, bc, dtT = _in_proj(x, norm_g, w_xbc, w_rest, w_dtT, layer, conv_w, conv_b)
    ys = _ssd(xsT, bc, dtT, dt_bias, a_log, d_skip)
    return _mix(x, ys, rest, ssd_norm_g, mem, mem_norm_g, *tables, *mix_weights, layer, final_g,
                apply_final_norm)


def kernel(x, mem, norm_g, w_in, conv_w, conv_b, dt_bias, a_log, d_skip, ssd_norm_g, mem_norm_g,
           w_kv, p_ssd, p_fno, p_mem, w_out, final_g):
    depth = norm_g.shape[0]
    tables = _dft_tables(x.shape[1])
    w_in_parts = _split_w_in(w_in)
    mix_weights = tuple(w.astype(BF16) for w in (w_kv, p_ssd, p_fno, p_mem, w_out))
    for i in range(depth):
        x = _layer(x, mem, i, norm_g[i], w_in_parts, conv_w[i], conv_b[i], dt_bias[i], a_log[i],
                   d_skip[i], ssd_norm_g[i], mem_norm_g[i], mix_weights, final_g, tables,
                   i == depth - 1)
    return x
```
